```python
import jax, jax.numpy as jnp
from jax import lax
import numpy as np

D_MODEL = 1024
BATCH = 4
SEQ = 4096
DEPTH = 4
DEC_BATCH = 128
DEC_SEQ = 1
PAST_LEN = 2048
PAGE_SIZE = 128

N_A = DEPTH // 2
N_B = DEPTH - N_A
N_HEADS = 16
HEAD_DIM = D_MODEL // N_HEADS
CONV_W = 3
D_FF = -(-8 * D_MODEL // (3 * 256)) * 256
Q_BLOCK = 128
ALPHA = (2 * DEPTH) ** 0.25
BETA = (8 * DEPTH) ** -0.25
LN_EPS = 1e-5
ATTN_SCALE = HEAD_DIM ** -0.5
FORGET_BIAS = 3.0

kernel_name = "yoco_shortconv_fox_decoder_step"


def _layer_norm(x, g, b):
    xf = x.astype(jnp.float32)
    mu = jnp.mean(xf, axis=-1, keepdims=True)
    var = jnp.mean(jnp.square(xf - mu), axis=-1, keepdims=True)
    return ((xf - mu) * lax.rsqrt(var + LN_EPS) * g.astype(jnp.float32) + b.astype(jnp.float32)).astype(x.dtype)


def _swiglu(x, w_gu, w_down):
    g, u = jnp.split(x @ w_gu, 2, axis=-1)
    return (jax.nn.silu(g) * u) @ w_down


def _short_conv(x, hist, w_in, conv_w, w_out):
    T = x.shape[1]
    b_gate, c_gate, h = jnp.split(x @ w_in, 3, axis=-1)
    u = c_gate * h
    ext = jnp.concatenate([hist.astype(u.dtype), u], axis=1)
    z = conv_w[CONV_W - 1] * ext[:, CONV_W - 1:CONV_W - 1 + T]
    for j in range(CONV_W - 1):
        z = z + conv_w[j] * ext[:, j:j + T]
    y = (b_gate * z) @ w_out
    return y, ext[:, -(CONV_W - 1):]


def _shared_kv(x, kv_w, f_w, f_b):
    Bn, T, _ = x.shape
    k, v = jnp.split(x @ kv_w, 2, axis=-1)
    k = k.reshape(Bn, T, N_HEADS, HEAD_DIM)
    v = v.reshape(Bn, T, N_HEADS, HEAD_DIM)
    logf = jax.nn.log_sigmoid((x @ f_w + f_b).astype(jnp.float32))
    return k, v, logf


def _attend(q, k, v, cum_q, cum_kT, q_pos):
    s = jnp.einsum("bqhd,bkhd->bhqk", q, k).astype(jnp.float32) * ATTN_SCALE
    s = s + jnp.swapaxes(cum_q, 1, 2)[..., None] - cum_kT[:, :, None, :]
    k_pos = jnp.arange(k.shape[1])
    s = jnp.where(k_pos[None, :] <= q_pos[:, None], s, -jnp.inf)
    p = jax.nn.softmax(s, axis=-1).astype(v.dtype)
    return jnp.einsum("bhqk,bkhd->bqhd", p, v)


def _fox_prompt(x, q_w, o_w, k, v, cum):
    Bn, T, _ = x.shape
    nb = T // Q_BLOCK
    q = (x @ q_w).reshape(Bn, nb, Q_BLOCK, N_HEADS, HEAD_DIM).swapaxes(0, 1)
    cq = cum.reshape(Bn, nb, Q_BLOCK, N_HEADS).swapaxes(0, 1)
    cum_kT = jnp.swapaxes(cum, 1, 2)

    def block(args):
        qb, cqb, i = args
        return _attend(qb, k, v, cqb, cum_kT, i * Q_BLOCK + jnp.arange(Q_BLOCK))

    o = lax.map(block, (q, cq, jnp.arange(nb)))
    o = o.swapaxes(0, 1).reshape(Bn, T, D_MODEL)
    return o @ o_w


def _fox_sample(x, q_w, o_w, k_all, v_all, cum_all):
    Bn, T, _ = x.shape
    L = k_all.shape[1]
    q = (x @ q_w).reshape(Bn, T, N_HEADS, HEAD_DIM)
    o = _attend(q, k_all, v_all, cum_all[:, L - T:], jnp.swapaxes(cum_all, 1, 2), L - T + jnp.arange(T))
    return o.reshape(Bn, T, D_MODEL) @ o_w


def _trunk(x, conv_hist, past, a_w_in, a_conv_w, a_w_out, b_w_q, b_w_o, kv_w, f_w, f_b,
           ffn_w_gu, ffn_w_down, ln_g, ln_b):
    new_hist = []
    for l in range(DEPTH):
        if l < N_A:
            y, h = _short_conv(x, conv_hist[l], a_w_in[l], a_conv_w[l], a_w_out[l])
            new_hist.append(h)
        else:
            j = l - N_A
            if j == 0:
                k_new, v_new, logf_new = _shared_kv(x, kv_w, f_w, f_b)
                if past is None:
                    k_all, v_all, logf_all = k_new, v_new, logf_new
                else:
                    pk, pv, plf = past
                    k_all = jnp.concatenate([pk.astype(k_new.dtype), k_new], axis=1)
                    v_all = jnp.concatenate([pv.astype(v_new.dtype), v_new], axis=1)
                    logf_all = jnp.concatenate([plf.astype(jnp.float32), logf_new], axis=1)
                cum = lax.cumsum(logf_all, axis=1)
            if past is None:
                y = _fox_prompt(x, b_w_q[j], b_w_o[j], k_all, v_all, cum)
            else:
                y = _fox_sample(x, b_w_q[j], b_w_o[j], k_all, v_all, cum)
        x = _layer_norm(ALPHA * x + y, ln_g[l, 0], ln_b[l, 0])
        x = _layer_norm(ALPHA * x + _swiglu(x, ffn_w_gu[l], ffn_w_down[l]), ln_g[l, 1], ln_b[l, 1])
    return x, k_new, v_new, logf_new.astype(k_new.dtype), jnp.stack(new_hist)


def setup_inputs(seed: int = 0) -> dict:
    key = jax.random.key(seed)
    ks = jax.random.split(key, 20)
    n_pages = PAST_LEN // PAGE_SIZE
    n_used = DEC_BATCH * n_pages
    n_pool = (5 * n_used + 3) // 4

    def nrm(k, shape, s):
        return jax.random.normal(k, shape, jnp.float32) * s

    page_table = jax.random.permutation(ks[0], n_pool)[:n_used].reshape(DEC_BATCH, n_pages).astype(jnp.int32)
    d_inv = D_MODEL ** -0.5
    return {
        "x_prompt": nrm(ks[1], (BATCH, SEQ, D_MODEL), 1.0),
        "x_sample": nrm(ks[2], (DEC_BATCH, DEC_SEQ, D_MODEL), 1.0),
        "cache_k": nrm(ks[3], (n_pool, PAGE_SIZE, N_HEADS, HEAD_DIM), 1.0),
        "cache_v": nrm(ks[4], (n_pool, PAGE_SIZE, N_HEADS, HEAD_DIM), 1.0),
        "cache_logf": jax.nn.log_sigmoid(FORGET_BIAS + nrm(ks[5], (n_pool, PAGE_SIZE, N_HEADS), 1.0)),
        "state_conv": nrm(ks[6], (N_A, DEC_BATCH, CONV_W - 1, D_MODEL), 1.0),
        "page_table": page_table,
        "a_w_in": nrm(ks[7], (N_A, D_MODEL, 3 * D_MODEL), d_inv),
        "a_conv_w": nrm(ks[8], (N_A, CONV_W, D_MODEL), CONV_W ** -0.5),
        "a_w_out": nrm(ks[9], (N_A, D_MODEL, D_MODEL), BETA * d_inv),
        "b_w_q": nrm(ks[10], (N_B, D_MODEL, D_MODEL), d_inv),
        "b_w_o": nrm(ks[11], (N_B, D_MODEL, D_MODEL), BETA * d_inv),
        "kv_w": nrm(ks[12], (D_MODEL, 2 * D_MODEL), d_inv),
        "f_w": nrm(ks[13], (D_MODEL, N_HEADS), d_inv),
        "f_b": FORGET_BIAS + nrm(ks[14], (N_HEADS,), 0.1),
        "ffn_w_gu": nrm(ks[15], (DEPTH, D_MODEL, 2 * D_FF), d_inv),
        "ffn_w_down": nrm(ks[16], (DEPTH, D_FF, D_MODEL), BETA * D_FF ** -0.5),
        "ln_g": 1.0 + nrm(ks[17], (DEPTH, 2, D_MODEL), 0.02),
        "ln_b": nrm(ks[18], (DEPTH, 2, D_MODEL), 0.02),
    }


def reference(x_prompt, x_sample, cache_k, cache_v, cache_logf, state_conv, page_table,
              a_w_in, a_conv_w, a_w_out, b_w_q, b_w_o, kv_w, f_w, f_b,
              ffn_w_gu, ffn_w_down, ln_g, ln_b):
    hist0 = jnp.zeros((N_A, x_prompt.shape[0], CONV_W - 1, D_MODEL), x_prompt.dtype)
    y_prompt, k_prompt, v_prompt, logf_prompt, conv_prompt = _trunk(
        x_prompt, hist0, None, a_w_in, a_conv_w, a_w_out, b_w_q, b_w_o, kv_w, f_w, f_b,
        ffn_w_gu, ffn_w_down, ln_g, ln_b)
    db = x_sample.shape[0]
    past_k = cache_k[page_table].reshape(db, -1, N_HEADS, HEAD_DIM)
    past_v = cache_v[page_table].reshape(db, -1, N_HEADS, HEAD_DIM)
    past_lf = cache_logf[page_table].reshape(db, -1, N_HEADS)
    y_sample, k_sample, v_sample, logf_sample, conv_sample = _trunk(
        x_sample, state_conv, (past_k, past_v, past_lf), a_w_in, a_conv_w, a_w_out, b_w_q, b_w_o,
        kv_w, f_w, f_b, ffn_w_gu, ffn_w_down, ln_g, ln_b)
    return (y_prompt, y_sample, k_prompt, v_prompt, logf_prompt, conv_prompt,
            k_sample, v_sample, logf_sample, conv_sample)
```

```python
import functools

import jax
import jax.numpy as jnp
from jax import lax
from jax.experimental import pallas as pl
from jax.experimental.pallas import tpu as pltpu

F32 = jnp.float32
BF16 = jnp.bfloat16
LN_EPS = 1e-5
V7X_VMEM_LIMIT_BYTES = 56 * 1024 * 1024
LANES = 128

_NT = (((1,), (1,)), ((), ()))


def _cparams(n_axes):
    return pltpu.CompilerParams(
        dimension_semantics=("arbitrary",) * n_axes,
        vmem_limit_bytes=V7X_VMEM_LIMIT_BYTES,
    )


def _const_spec(shape):
    nd = len(shape)
    return pl.BlockSpec(shape, lambda *_: (0,) * nd, pipeline_mode=pl.Buffered(1))


def _layer_norm(r, g, b):
    mu = jnp.mean(r, axis=-1, keepdims=True)
    c = r - mu
    var = jnp.mean(c * c, axis=-1, keepdims=True)
    return c * lax.rsqrt(var + LN_EPS) * g + b


def _gated_conv_out(x, bch, p1, p2, cw_ref, w_out_ref, g_ref, b_ref, alpha):
    d = x.shape[-1]
    u = bch[:, d:2 * d] * bch[:, 2 * d:]
    z = cw_ref[2:3, :] * u
    z = z + cw_ref[0:1, :] * p2
    z = z + cw_ref[1:2, :] * p1
    y = jnp.dot((bch[:, :d] * z).astype(BF16), w_out_ref[...], preferred_element_type=F32)
    return _layer_norm(alpha * x + y, g_ref[...], b_ref[...])


def _conv_seq_kernel(x_ref, hist_ref, w_in_ref, cw_ref, w_out_ref, g_ref, b_ref,
                     o_ref, st_ref, carry_ref, *, alpha):
    j = pl.program_id(1)

    @pl.when(j == 0)
    def _():
        carry_ref[...] = hist_ref[0]

    x = x_ref[0]
    tm, d = x.shape
    bch = jnp.dot(x.astype(BF16), w_in_ref[...], preferred_element_type=F32)
    u = bch[:, d:2 * d] * bch[:, 2 * d:]
    h0 = carry_ref[0:1, :]
    h1 = carry_ref[1:2, :]
    row = lax.broadcasted_iota(jnp.int32, (tm, d), 0)
    p1 = jnp.where(row == 0, h1, pltpu.roll(u, 1, 0))
    p2 = jnp.where(row == 0, h0, jnp.where(row == 1, h1, pltpu.roll(u, 2, 0)))
    tail = u[tm - 2:, :]
    carry_ref[...] = tail
    o_ref[0] = _gated_conv_out(x, bch, p1, p2, cw_ref, w_out_ref, g_ref, b_ref, alpha)

    @pl.when(j == pl.num_programs(1) - 1)
    def _():
        st_ref[0] = tail


def _conv_seq(x, hist, w_in, cw, w_out, g, b, *, alpha, tm):
    bsz, t, d = x.shape
    return pl.pallas_call(
        functools.partial(_conv_seq_kernel, alpha=alpha),
        grid=(bsz, t // tm),
        in_specs=[
            pl.BlockSpec((1, tm, d), lambda i, j: (i, j, 0)),
            pl.BlockSpec((1, 2, d), lambda i, j: (i, 0, 0)),
            _const_spec(w_in.shape), _const_spec(cw.shape), _const_spec(w_out.shape),
            _const_spec(g.shape), _const_spec(b.shape),
        ],
        out_specs=[
            pl.BlockSpec((1, tm, d), lambda i, j: (i, j, 0)),
            pl.BlockSpec((1, 2, d), lambda i, j: (i, 0, 0)),
        ],
        out_shape=[jax.ShapeDtypeStruct((bsz, t, d), F32), jax.ShapeDtypeStruct((bsz, 2, d), F32)],
        scratch_shapes=[pltpu.VMEM((2, d), F32)],
        compiler_params=_cparams(2),
        name="conv_seq",
    )(x, hist, w_in, cw, w_out, g, b)


def _conv_tok_kernel(x_ref, hist_ref, w_in_ref, cw_ref, w_out_ref, g_ref, b_ref,
                     o_ref, st_ref, *, alpha):
    x = x_ref[...]
    d = x.shape[-1]
    bch = jnp.dot(x.astype(BF16), w_in_ref[...], preferred_element_type=F32)
    u = bch[:, d:2 * d] * bch[:, 2 * d:]
    h0 = hist_ref[:, :d]
    h1 = hist_ref[:, d:]
    o_ref[...] = _gated_conv_out(x, bch, h1, h0, cw_ref, w_out_ref, g_ref, b_ref, alpha)
    st_ref[:, :d] = h1
    st_ref[:, d:] = u


def _conv_tok(x, hist, w_in, cw, w_out, g, b, *, alpha):
    n, d = x.shape
    args = (x, hist, w_in, cw, w_out, g, b)
    return pl.pallas_call(
        functools.partial(_conv_tok_kernel, alpha=alpha),
        grid=(1,),
        in_specs=[_const_spec(a.shape) for a in args],
        out_specs=[_const_spec((n, d)), _const_spec((n, 2 * d))],
        out_shape=[jax.ShapeDtypeStruct((n, d), F32), jax.ShapeDtypeStruct((n, 2 * d), F32)],
        compiler_params=_cparams(1),
        name="conv_tok",
    )(*args)


def _ffn_kernel(x_ref, wgu_ref, wd_ref, g_ref, b_ref, o_ref, *, alpha):
    x = x_ref[...]
    dff = wd_ref.shape[0]
    gu = jnp.dot(x.astype(BF16), wgu_ref[...], preferred_element_type=F32)
    gate = gu[:, :dff]
    a = (gate * jax.nn.sigmoid(gate)) * gu[:, dff:]
    y = jnp.dot(a.astype(BF16), wd_ref[...], preferred_element_type=F32)
    o_ref[...] = _layer_norm(alpha * x + y, g_ref[...], b_ref[...])


def _ffn(x, wgu, wd, g, b, *, alpha, tm):
    m, d = x.shape
    return pl.pallas_call(
        functools.partial(_ffn_kernel, alpha=alpha),
        grid=(m // tm,),
        in_specs=[
            pl.BlockSpec((tm, d), lambda i: (i, 0)),
            _const_spec(wgu.shape), _const_spec(wd.shape), _const_spec(g.shape), _const_spec(b.shape),
        ],
        out_specs=pl.BlockSpec((tm, d), lambda i: (i, 0)),
        out_shape=jax.ShapeDtypeStruct((m, d), F32),
        compiler_params=_cparams(1),
        name="ffn",
    )(x, wgu, wd, g, b)


def _kv_kernel(x_ref, wkv_ref, fw_ref, fb_ref, k_ref, v_ref, lf_ref, cum_ref, kb_ref, vb_ref,
               carry_ref, *, tiles_per_seq):
    i = pl.program_id(0)

    @pl.when(i % tiles_per_seq == 0)
    def _():
        carry_ref[...] = jnp.zeros_like(carry_ref)

    xb = x_ref[...].astype(BF16)
    tm, d = xb.shape
    nh = lf_ref.shape[-1]
    kv = jnp.dot(xb, wkv_ref[...], preferred_element_type=F32)
    k = kv[:, :d]
    v = kv[:, d:]
    k_ref[...] = k
    v_ref[...] = v
    kb_ref[...] = k.astype(BF16)
    vb_ref[...] = v.astype(BF16)
    z = jnp.dot(xb, fw_ref[...], preferred_element_type=F32) + fb_ref[...]
    lf = jnp.minimum(z, 0.0) - jnp.log1p(jnp.exp(-jnp.abs(z)))
    lf_ref[...] = lf[:, :nh]
    row = lax.broadcasted_iota(jnp.int32, lf.shape, 0)
    c = lf
    shift = 1
    while shift < tm:
        c = c + jnp.where(row >= shift, pltpu.roll(c, shift, 0), 0.0)
        shift *= 2
    c = c + carry_ref[...]
    carry_ref[...] = c[tm - 1:, :]
    cum_ref[...] = c[:, :nh]


def _kv(x, wkv, fw, fb, *, nh, tm, tiles_per_seq):
    m, d = x.shape
    row_spec = pl.BlockSpec((tm, d), lambda i: (i, 0))
    head_spec = pl.BlockSpec((tm, nh), lambda i: (i, 0))
    return pl.pallas_call(
        functools.partial(_kv_kernel, tiles_per_seq=tiles_per_seq),
        grid=(m // tm,),
        in_specs=[row_spec, _const_spec(wkv.shape), _const_spec(fw.shape), _const_spec(fb.shape)],
        out_specs=[row_spec, row_spec, head_spec, head_spec, row_spec, row_spec],
        out_shape=[
            jax.ShapeDtypeStruct((m, d), F32), jax.ShapeDtypeStruct((m, d), F32),
            jax.ShapeDtypeStruct((m, nh), F32), jax.ShapeDtypeStruct((m, nh), F32),
            jax.ShapeDtypeStruct((m, d), BF16), jax.ShapeDtypeStruct((m, d), BF16),
        ],
        scratch_shapes=[pltpu.VMEM((1, fw.shape[1]), F32)],
        compiler_params=_cparams(1),
        name="kv",
    )(x, wkv, fw, fb)


def _proj_kernel(x_ref, w_ref, o_ref, *, scale):
    y = jnp.dot(x_ref[...].astype(BF16), w_ref[...], preferred_element_type=F32)
    o_ref[...] = (y * scale).astype(o_ref.dtype)


def _proj(x, w, *, scale, out_dtype, tm):
    m, d = x.shape
    n = w.shape[1]
    return pl.pallas_call(
        functools.partial(_proj_kernel, scale=scale),
        grid=(m // tm,),
        in_specs=[pl.BlockSpec((tm, d), lambda i: (i, 0)), _const_spec(w.shape)],
        out_specs=pl.BlockSpec((tm, n), lambda i: (i, 0)),
        out_shape=jax.ShapeDtypeStruct((m, n), out_dtype),
        compiler_params=_cparams(1),
        name="proj",
    )(x, w)


def _proj_ln_kernel(a_ref, x_ref, w_ref, g_ref, b_ref, o_ref, *, alpha):
    y = jnp.dot(a_ref[...].astype(BF16), w_ref[...], preferred_element_type=F32)
    o_ref[...] = _layer_norm(alpha * x_ref[...] + y, g_ref[...], b_ref[...])


def _proj_ln(a, x, w, g, b, *, alpha, tm):
    m, d = x.shape
    row = lambda i: (i, 0)
    return pl.pallas_call(
        functools.partial(_proj_ln_kernel, alpha=alpha),
        grid=(m // tm,),
        in_specs=[pl.BlockSpec((tm, a.shape[1]), row), pl.BlockSpec((tm, d), row),
                  _const_spec(w.shape), _const_spec(g.shape), _const_spec(b.shape)],
        out_specs=pl.BlockSpec((tm, d), row),
        out_shape=jax.ShapeDtypeStruct((m, d), F32),
        compiler_params=_cparams(1),
        name="proj_ln",
    )(a, x, w, g, b)


def _fa_kernel(q_ref, k_ref, v_ref, cq_ref, ckt_ref, o_ref, *, tq, nh, dh):
    i = pl.program_id(1)
    heads_per_tile = LANES // dh
    lane = lax.broadcasted_iota(jnp.int32, (tq, LANES), 1)
    qpos = lax.broadcasted_iota(jnp.int32, (tq, tq), 0)
    kpos = lax.broadcasted_iota(jnp.int32, (tq, tq), 1)

    for gidx in range(nh // heads_per_tile):
        lanes = slice(gidx * LANES, (gidx + 1) * LANES)
        q2 = q_ref[0, :, lanes]
        out = None
        for jh in range(heads_per_tile):
            h = gidx * heads_per_tile + jh
            in_head = (lane >= jh * dh) & (lane < (jh + 1) * dh)
            qm = jnp.where(in_head, q2, jnp.zeros_like(q2))
            cq = cq_ref[0, :, h:h + 1]

            def block(kb, carry, masked, qm=qm, cq=cq, h=h, lanes=lanes):
                m, l, acc = carry
                ks = pl.multiple_of(kb * tq, tq)
                k2 = k_ref[0, pl.ds(ks, tq), lanes]
                v2 = v_ref[0, pl.ds(ks, tq), lanes]
                ck = ckt_ref[0, h:h + 1, pl.ds(ks, tq)]
                s = lax.dot_general(qm, k2, _NT, preferred_element_type=F32)
                s = s + cq - ck
                if masked:
                    s = jnp.where(kpos <= qpos, s, -jnp.inf)
                m_new = jnp.maximum(m, jnp.max(s, axis=-1, keepdims=True))
                a = jnp.exp(m - m_new)
                p = jnp.exp(s - m_new)
                l = a * l + jnp.sum(p, axis=-1, keepdims=True)
                acc = a * acc + jnp.dot(p.astype(BF16), v2, preferred_element_type=F32)
                return m_new, l, acc

            init = (jnp.full((tq, 1), -jnp.inf, F32), jnp.zeros((tq, 1), F32), jnp.zeros((tq, LANES), F32))
            carry = lax.fori_loop(0, i, functools.partial(block, masked=False), init)
            _, l, acc = block(i, carry, True)
            o_h = acc / l
            out = o_h if out is None else jnp.where(in_head, o_h, out)
        o_ref[0, :, lanes] = out.astype(o_ref.dtype)


def _flash_attn(q, kb, vb, cum, cum_t, *, nh, tq):
    bsz, t, d = q.shape
    dh = d // nh
    return pl.pallas_call(
        functools.partial(_fa_kernel, tq=tq, nh=nh, dh=dh),
        grid=(bsz, t // tq),
        in_specs=[
            pl.BlockSpec((1, tq, d), lambda b, i: (b, i, 0)),
            pl.BlockSpec((1, t, d), lambda b, i: (b, 0, 0)),
            pl.BlockSpec((1, t, d), lambda b, i: (b, 0, 0)),
            pl.BlockSpec((1, tq, nh), lambda b, i: (b, i, 0)),
            pl.BlockSpec((1, nh, t), lambda b, i: (b, 0, 0)),
        ],
        out_specs=pl.BlockSpec((1, tq, d), lambda b, i: (b, i, 0)),
        out_shape=jax.ShapeDtypeStruct((bsz, t, d), BF16),
        compiler_params=_cparams(2),
        name="flash_attn",
    )(q, kb, vb, cum, cum_t)


def _decode_kernel(pt_ref, q_ref, kn_ref, vn_ref, lfn_ref, *refs, pages, nh, dh):
    del pt_ref
    k_refs = refs[:pages]
    v_refs = refs[pages:2 * pages]
    lf_refs = refs[2 * pages:3 * pages]
    o_ref = refs[3 * pages]
    qbd_ref, m_ref, l_ref, acc_ref, r_ref = refs[3 * pages + 1:]
    c = pl.program_id(1)
    d = nh * dh
    page = k_refs[0].shape[1]
    head_row = lax.broadcasted_iota(jnp.int32, (nh, d), 0)
    head_of_lane = lax.broadcasted_iota(jnp.int32, (nh, d), 1) // dh
    block_diag = head_of_lane == head_row
    eye = (lax.broadcasted_iota(jnp.int32, (nh, nh), 0)
           == lax.broadcasted_iota(jnp.int32, (nh, nh), 1))

    @pl.when(c == 0)
    def _():
        qbd = jnp.where(block_diag, jnp.broadcast_to(q_ref[0], (nh, d)), 0.0)
        qbd_ref[...] = qbd.astype(BF16)
        m_ref[...] = jnp.sum(qbd * kn_ref[0], axis=-1, keepdims=True)
        l_ref[...] = jnp.ones_like(l_ref)
        acc_ref[...] = jnp.broadcast_to(vn_ref[0], (nh, d))
        lfn = jnp.broadcast_to(lfn_ref[0], (nh, nh))
        r_ref[...] = jnp.sum(jnp.where(eye, lfn, 0.0), axis=-1, keepdims=True)

    qbd = qbd_ref[...]
    m = m_ref[...]
    l = l_ref[...]
    acc = acc_ref[...]
    r = r_ref[...]
    key_lane = lax.broadcasted_iota(jnp.int32, (nh, page), 1)
    eye_f = eye.astype(F32)
    for pg in reversed(range(pages)):
        s = lax.dot_general(qbd, k_refs[pg][0].astype(BF16), _NT, preferred_element_type=F32)
        lft = lax.dot_general(eye_f, lf_refs[pg][0], _NT, preferred_element_type=F32,
                              precision=lax.Precision.HIGHEST)
        inc = lft
        shift = 1
        while shift < page:
            moved = pltpu.roll(inc, page - shift, 1)
            inc = inc + jnp.where(key_lane < page - shift, moved, 0.0)
            shift *= 2
        s = s + ((inc - lft) + r)
        r = r + inc[:, 0:1]
        m_new = jnp.maximum(m, jnp.max(s, axis=-1, keepdims=True))
        a = jnp.exp(m - m_new)
        p = jnp.exp(s - m_new)
        l = a * l + jnp.sum(p, axis=-1, keepdims=True)
        acc = a * acc + jnp.dot(p.astype(BF16), v_refs[pg][0].astype(BF16), preferred_element_type=F32)
        m = m_new
    m_ref[...] = m
    l_ref[...] = l
    acc_ref[...] = acc
    r_ref[...] = r

    @pl.when(c == pl.num_programs(1) - 1)
    def _():
        o_ref[0] = jnp.sum(jnp.where(block_diag, acc / l, 0.0), axis=0, keepdims=True)


def _decode_attn(page_table, q, k_new, v_new, lf_new, cache_k, cache_v, cache_lf, *, nh, pages):
    n, d = q.shape
    dh = d // nh
    n_pool, page = cache_lf.shape[0], cache_lf.shape[1]
    n_chunks = page_table.shape[1] // pages
    ck = cache_k.reshape(n_pool, page, d)
    cv = cache_v.reshape(n_pool, page, d)

    def tok_spec(width):
        return pl.BlockSpec((1, 1, width), lambda b, c, pt: (b, 0, 0))

    def page_spec(width, pg):
        return pl.BlockSpec(
            (1, page, width),
            lambda b, c, pt: (pt[b, (n_chunks - 1 - c) * pages + pg], 0, 0))

    in_specs = [tok_spec(d), tok_spec(d), tok_spec(d), tok_spec(nh)]
    in_specs += [page_spec(d, pg) for pg in range(pages)]
    in_specs += [page_spec(d, pg) for pg in range(pages)]
    in_specs += [page_spec(nh, pg) for pg in range(pages)]
    grid_spec = pltpu.PrefetchScalarGridSpec(
        num_scalar_prefetch=1,
        grid=(n, n_chunks),
        in_specs=in_specs,
        out_specs=tok_spec(d),
        scratch_shapes=[
            pltpu.VMEM((nh, d), BF16), pltpu.VMEM((nh, 1), F32), pltpu.VMEM((nh, 1), F32),
            pltpu.VMEM((nh, d), F32), pltpu.VMEM((nh, 1), F32),
        ],
    )
    out = pl.pallas_call(
        functools.partial(_decode_kernel, pages=pages, nh=nh, dh=dh),
        grid_spec=grid_spec,
        out_shape=jax.ShapeDtypeStruct((n, 1, d), F32),
        compiler_params=_cparams(2),
        name="decode_attn",
    )(page_table, q.reshape(n, 1, d), k_new.reshape(n, 1, d), v_new.reshape(n, 1, d),
      lf_new.reshape(n, 1, nh), *([ck] * pages), *([cv] * pages), *([cache_lf] * pages))
    return out.reshape(n, d)


def kernel(x_prompt, x_sample, cache_k, cache_v, cache_logf, state_conv, page_table, a_w_in, a_conv_w, a_w_out, b_w_q, b_w_o, kv_w, f_w, f_b, ffn_w_gu, ffn_w_down, ln_g, ln_b):
    bsz, t, d = x_prompt.shape
    n_s = x_sample.shape[0]
    n_a, n_b = a_w_in.shape[0], b_w_q.shape[0]
    depth = n_a + n_b
    nh = f_w.shape[1]
    dh = d // nh
    alpha = (2 * depth) ** 0.25
    attn_scale = dh ** -0.5
    m = bsz * t
    tm_seq, tm_ffn, tq, dec_pages = 512, 256, 512, 4

    w_in, w_out = a_w_in.astype(BF16), a_w_out.astype(BF16)
    w_q, w_o, w_kv = b_w_q.astype(BF16), b_w_o.astype(BF16), kv_w.astype(BF16)
    w_gu, w_down = ffn_w_gu.astype(BF16), ffn_w_down.astype(BF16)
    fw = jnp.pad(f_w, ((0, 0), (0, LANES - nh))).astype(BF16)
    fb = jnp.pad(f_b, (0, LANES - nh)).reshape(1, LANES)
    g = ln_g.reshape(depth, 2, 1, d)
    b = ln_b.reshape(depth, 2, 1, d)

    x = x_prompt
    hist0 = jnp.zeros((bsz, 2, d), F32)
    conv_p = []
    for l in range(n_a):
        x, st = _conv_seq(x, hist0, w_in[l], a_conv_w[l], w_out[l], g[l, 0], b[l, 0], alpha=alpha, tm=tm_seq)
        conv_p.append(st)
        x = _ffn(x.reshape(m, d), w_gu[l], w_down[l], g[l, 1], b[l, 1], alpha=alpha, tm=tm_ffn).reshape(bsz, t, d)
    x = x.reshape(m, d)
    k_p, v_p, lf_p, cum, kb, vb = _kv(x, w_kv, fw, fb, nh=nh, tm=tm_seq, tiles_per_seq=t // tm_seq)
    cum = cum.reshape(bsz, t, nh)
    cum_t = jnp.swapaxes(cum, 1, 2)
    kb = kb.reshape(bsz, t, d)
    vb = vb.reshape(bsz, t, d)
    for j in range(n_b):
        l = n_a + j
        q = _proj(x, w_q[j], scale=attn_scale, out_dtype=BF16, tm=tm_seq)
        o = _flash_attn(q.reshape(bsz, t, d), kb, vb, cum, cum_t, nh=nh, tq=tq)
        x = _proj_ln(o.reshape(m, d), x, w_o[j], g[l, 0], b[l, 0], alpha=alpha, tm=tm_seq)
        x = _ffn(x, w_gu[l], w_down[l], g[l, 1], b[l, 1], alpha=alpha, tm=tm_ffn)
    y_prompt = x.reshape(bsz, t, d)

    xs = x_sample.reshape(n_s, d)
    conv_s = []
    for l in range(n_a):
        xs, st = _conv_tok(xs, state_conv[l].reshape(n_s, 2 * d), w_in[l], a_conv_w[l], w_out[l],
                           g[l, 0], b[l, 0], alpha=alpha)
        conv_s.append(st.reshape(n_s, 2, d))
        xs = _ffn(xs, w_gu[l], w_down[l], g[l, 1], b[l, 1], alpha=alpha, tm=n_s)
    k_s, v_s, lf_s, _, _, _ = _kv(xs, w_kv, fw, fb, nh=nh, tm=n_s, tiles_per_seq=1)
    for j in range(n_b):
        l = n_a + j
        qs = _proj(xs, w_q[j], scale=attn_scale, out_dtype=F32, tm=n_s)
        os_ = _decode_attn(page_table, qs, k_s, v_s, lf_s, cache_k, cache_v, cache_logf, nh=nh, pages=dec_pages)
        xs = _proj_ln(os_, xs, w_o[j], g[l, 0], b[l, 0], alpha=alpha, tm=n_s)
        xs = _ffn(xs, w_gu[l], w_down[l], g[l, 1], b[l, 1], alpha=alpha, tm=n_s)

    return (y_prompt, xs.reshape(n_s, 1, d),
            k_p.reshape(bsz, t, nh, dh), v_p.reshape(bsz, t, nh, dh), lf_p.reshape(bsz, t, nh),
            jnp.stack(conv_p),
            k_s.reshape(n_s, 1, nh, dh), v_s.reshape(n_s, 1, nh, dh), lf_s.reshape(n_s, 1, nh),
            jnp.stack(conv_s))
```

```python
import functools
import math

import numpy as np
import jax
import jax.numpy as jnp
from jax import lax
from jax.experimental import pallas as pl
from jax.experimental.pallas import tpu as pltpu

F32 = jnp.float32
BF16 = jnp.bfloat16
LN_EPS = 1e-5
LOG2E = math.log2(math.e)
V7X_VMEM_LIMIT_BYTES = 56 * 1024 * 1024
LANES = 128
N_SPLIT = 3
AUG_PER_HEAD = 2 * N_SPLIT

_NT = (((1,), (1,)), ((), ()))


def _cparams(n_axes):
    return pltpu.CompilerParams(
        dimension_semantics=("arbitrary",) * n_axes,
        vmem_limit_bytes=V7X_VMEM_LIMIT_BYTES,
    )


def _const_spec(shape):
    nd = len(shape)
    return pl.BlockSpec(shape, lambda *_: (0,) * nd, pipeline_mode=pl.Buffered(1))


def _layer_norm(r, g, b):
    mu = jnp.mean(r, axis=-1, keepdims=True)
    c = r - mu
    var = jnp.mean(c * c, axis=-1, keepdims=True)
    return c * lax.rsqrt(var + LN_EPS) * g + b


def _log_sigmoid(z):
    return jnp.minimum(z, 0.0) - jnp.log1p(jnp.exp(-jnp.abs(z)))


def _split3(c):
    hi = c.astype(BF16).astype(F32)
    r = c - hi
    mid = r.astype(BF16).astype(F32)
    lo = (r - mid).astype(BF16).astype(F32)
    return hi, mid, lo


def _gated_conv_out(x, bch, p1, p2, cw_ref, w_out_ref, g_ref, b_ref, alpha):
    d = x.shape[-1]
    u = bch[:, d:2 * d] * bch[:, 2 * d:]
    z = cw_ref[2:3, :] * u
    z = z + cw_ref[0:1, :] * p2
    z = z + cw_ref[1:2, :] * p1
    y = jnp.dot((bch[:, :d] * z).astype(BF16), w_out_ref[...], preferred_element_type=F32)
    return _layer_norm(alpha * x + y, g_ref[...], b_ref[...])


def _conv_seq_kernel(x_ref, hist_ref, w_in_ref, cw_ref, w_out_ref, g_ref, b_ref,
                     o_ref, st_ref, carry_ref, *, alpha):
    j = pl.program_id(1)

    @pl.when(j == 0)
    def _():
        carry_ref[...] = hist_ref[0]

    x = x_ref[0]
    tm, d = x.shape
    bch = jnp.dot(x.astype(BF16), w_in_ref[...], preferred_element_type=F32)
    u = bch[:, d:2 * d] * bch[:, 2 * d:]
    h0 = carry_ref[0:1, :]
    h1 = carry_ref[1:2, :]
    row = lax.broadcasted_iota(jnp.int32, (tm, d), 0)
    p1 = jnp.where(row == 0, h1, pltpu.roll(u, 1, 0))
    p2 = jnp.where(row == 0, h0, jnp.where(row == 1, h1, pltpu.roll(u, 2, 0)))
    tail = u[tm - 2:, :]
    carry_ref[...] = tail
    o_ref[0] = _gated_conv_out(x, bch, p1, p2, cw_ref, w_out_ref, g_ref, b_ref, alpha)

    @pl.when(j == pl.num_programs(1) - 1)
    def _():
        st_ref[0] = tail


def _conv_seq(x, hist, w_in, cw, w_out, g, b, *, alpha, tm):
    bsz, t, d = x.shape
    return pl.pallas_call(
        functools.partial(_conv_seq_kernel, alpha=alpha),
        grid=(bsz, t // tm),
        in_specs=[
            pl.BlockSpec((1, tm, d), lambda i, j: (i, j, 0)),
            pl.BlockSpec((1, 2, d), lambda i, j: (i, 0, 0)),
            _const_spec(w_in.shape), _const_spec(cw.shape), _const_spec(w_out.shape),
            _const_spec(g.shape), _const_spec(b.shape),
        ],
        out_specs=[
            pl.BlockSpec((1, tm, d), lambda i, j: (i, j, 0)),
            pl.BlockSpec((1, 2, d), lambda i, j: (i, 0, 0)),
        ],
        out_shape=[jax.ShapeDtypeStruct((bsz, t, d), F32), jax.ShapeDtypeStruct((bsz, 2, d), F32)],
        scratch_shapes=[pltpu.VMEM((2, d), F32)],
        compiler_params=_cparams(2),
        name="conv_seq",
    )(x, hist, w_in, cw, w_out, g, b)


def _conv_tok_kernel(x_ref, hist_ref, w_in_ref, cw_ref, w_out_ref, g_ref, b_ref,
                     o_ref, st_ref, *, alpha):
    x = x_ref[...]
    d = x.shape[-1]
    bch = jnp.dot(x.astype(BF16), w_in_ref[...], preferred_element_type=F32)
    u = bch[:, d:2 * d] * bch[:, 2 * d:]
    h0 = hist_ref[:, :d]
    h1 = hist_ref[:, d:]
    o_ref[...] = _gated_conv_out(x, bch, h1, h0, cw_ref, w_out_ref, g_ref, b_ref, alpha)
    st_ref[:, :d] = h1
    st_ref[:, d:] = u


def _conv_tok(x, hist, w_in, cw, w_out, g, b, *, alpha):
    n, d = x.shape
    args = (x, hist, w_in, cw, w_out, g, b)
    return pl.pallas_call(
        functools.partial(_conv_tok_kernel, alpha=alpha),
        grid=(1,),
        in_specs=[_const_spec(a.shape) for a in args],
        out_specs=[_const_spec((n, d)), _const_spec((n, 2 * d))],
        out_shape=[jax.ShapeDtypeStruct((n, d), F32), jax.ShapeDtypeStruct((n, 2 * d), F32)],
        compiler_params=_cparams(1),
        name="conv_tok",
    )(*args)


def _ffn_kernel(x_ref, wgu_ref, wd_ref, g_ref, b_ref, o_ref, *, alpha):
    x = x_ref[...]
    dff = wd_ref.shape[0]
    gu = jnp.dot(x.astype(BF16), wgu_ref[...], preferred_element_type=F32)
    gate = gu[:, :dff]
    a = (gate * jax.nn.sigmoid(gate)) * gu[:, dff:]
    y = jnp.dot(a.astype(BF16), wd_ref[...], preferred_element_type=F32)
    o_ref[...] = _layer_norm(alpha * x + y, g_ref[...], b_ref[...])


def _ffn(x, wgu, wd, g, b, *, alpha, tm):
    m, d = x.shape
    return pl.pallas_call(
        functools.partial(_ffn_kernel, alpha=alpha),
        grid=(m // tm,),
        in_specs=[
            pl.BlockSpec((tm, d), lambda i: (i, 0)),
            _const_spec(wgu.shape), _const_spec(wd.shape), _const_spec(g.shape), _const_spec(b.shape),
        ],
        out_specs=pl.BlockSpec((tm, d), lambda i: (i, 0)),
        out_shape=jax.ShapeDtypeStruct((m, d), F32),
        compiler_params=_cparams(1),
        name="ffn",
    )(x, wgu, wd, g, b)


def _kv_seq_kernel(x_ref, wkvt_ref, wk_ref, fw_ref, fb_ref, fwt_ref, fbt_ref, place_ref,
                   kt_ref, vt_ref, lft_ref, kb_ref, ak_ref, vtb_ref, ct_ref,
                   carry_ref, carry_t_ref):
    j = pl.program_id(1)

    @pl.when(j == 0)
    def _():
        carry_ref[...] = jnp.zeros_like(carry_ref)
        carry_t_ref[...] = jnp.zeros_like(carry_t_ref)

    xb = x_ref[0].astype(BF16)
    tm, d = xb.shape
    nh = lft_ref.shape[1]
    kvt = lax.dot_general(wkvt_ref[...], xb, _NT, preferred_element_type=F32)
    kt_ref[0] = kvt[:d].reshape(kt_ref.shape[1:])
    vt_ref[0] = kvt[d:].reshape(vt_ref.shape[1:])
    vtb_ref[0] = kvt[d:].astype(BF16)
    kb_ref[0] = jnp.dot(xb, wk_ref[...], preferred_element_type=F32).astype(BF16)

    lft = _log_sigmoid(lax.dot_general(fwt_ref[...], xb, _NT, preferred_element_type=F32) + fbt_ref[...])
    lft_ref[0] = lft
    lane = lax.broadcasted_iota(jnp.int32, lft.shape, 1)
    ct = lft
    shift = 1
    while shift < tm:
        ct = ct + jnp.where(lane >= shift, pltpu.roll(ct, shift, 1), 0.0)
        shift *= 2
    ct = ct + carry_t_ref[...]
    carry_t_ref[...] = ct[:, tm - 1:]
    ct_ref[0] = ct * LOG2E

    lf = _log_sigmoid(jnp.dot(xb, fw_ref[...], preferred_element_type=F32) + fb_ref[...])
    row = lax.broadcasted_iota(jnp.int32, lf.shape, 0)
    c = lf
    shift = 1
    while shift < tm:
        c = c + jnp.where(row >= shift, pltpu.roll(c, shift, 0), 0.0)
        shift *= 2
    c = c + carry_ref[...]
    carry_ref[...] = c[tm - 1:, :]
    hi, mid, lo = _split3(c * LOG2E)
    grp = lax.broadcasted_iota(jnp.int32, lf.shape, 1) // nh
    pieces = jnp.where(grp == 0, hi, jnp.where(grp == 1, mid, jnp.where(grp == 2, lo,
                       jnp.where(grp == N_SPLIT, 1.0, 0.0))))
    ak_ref[0] = jnp.dot(pieces.astype(BF16), place_ref[...], preferred_element_type=F32).astype(BF16)


def _kv_seq(x, wkvt, wk, fw, fb, fwt, fbt, place, *, nh, tm):
    bsz, t, d = x.shape
    dh = d // nh
    tok = pl.BlockSpec((1, tm, d), lambda i, j: (i, j, 0))
    tr4 = pl.BlockSpec((1, nh, dh, tm), lambda i, j: (i, 0, 0, j))
    tr3 = pl.BlockSpec((1, nh, tm), lambda i, j: (i, 0, j))
    trd = pl.BlockSpec((1, d, tm), lambda i, j: (i, 0, j))
    consts = (wkvt, wk, fw, fb, fwt, fbt, place)
    return pl.pallas_call(
        _kv_seq_kernel,
        grid=(bsz, t // tm),
        in_specs=[tok] + [_const_spec(a.shape) for a in consts],
        out_specs=[tr4, tr4, tr3, tok, tok, trd, tr3],
        out_shape=[
            jax.ShapeDtypeStruct((bsz, nh, dh, t), F32), jax.ShapeDtypeStruct((bsz, nh, dh, t), F32),
            jax.ShapeDtypeStruct((bsz, nh, t), F32),
            jax.ShapeDtypeStruct((bsz, t, d), BF16), jax.ShapeDtypeStruct((bsz, t, d), BF16),
            jax.ShapeDtypeStruct((bsz, d, t), BF16), jax.ShapeDtypeStruct((bsz, nh, t), F32),
        ],
        scratch_shapes=[pltpu.VMEM((1, fw.shape[1]), F32), pltpu.VMEM((nh, 1), F32)],
        compiler_params=_cparams(2),
        name="kv_seq",
    )(x, *consts)


def _kv_tok_kernel(x_ref, wkv_ref, fw_ref, fb_ref, k_ref, v_ref, lf_ref):
    xb = x_ref[...].astype(BF16)
    d = xb.shape[1]
    nh = lf_ref.shape[-1]
    kv = jnp.dot(xb, wkv_ref[...], preferred_element_type=F32)
    k_ref[...] = kv[:, :d]
    v_ref[...] = kv[:, d:]
    lf = _log_sigmoid(jnp.dot(xb, fw_ref[...], preferred_element_type=F32) + fb_ref[...])
    lf_ref[...] = lf[:, :nh]


def _kv_tok(x, wkv, fw, fb, *, nh):
    n, d = x.shape
    args = (x, wkv, fw, fb)
    return pl.pallas_call(
        _kv_tok_kernel,
        grid=(1,),
        in_specs=[_const_spec(a.shape) for a in args],
        out_specs=[_const_spec((n, d)), _const_spec((n, d)), _const_spec((n, nh))],
        out_shape=[jax.ShapeDtypeStruct((n, d), F32), jax.ShapeDtypeStruct((n, d), F32),
                   jax.ShapeDtypeStruct((n, nh), F32)],
        compiler_params=_cparams(1),
        name="kv_tok",
    )(*args)


def _proj_kernel(x_ref, w_ref, o_ref, *, scale):
    y = jnp.dot(x_ref[...].astype(BF16), w_ref[...], preferred_element_type=F32)
    o_ref[...] = (y * scale).astype(o_ref.dtype)


def _proj(x, w, *, scale, out_dtype, tm):
    m, d = x.shape
    n = w.shape[1]
    return pl.pallas_call(
        functools.partial(_proj_kernel, scale=scale),
        grid=(m // tm,),
        in_specs=[pl.BlockSpec((tm, d), lambda i: (i, 0)), _const_spec(w.shape)],
        out_specs=pl.BlockSpec((tm, n), lambda i: (i, 0)),
        out_shape=jax.ShapeDtypeStruct((m, n), out_dtype),
        compiler_params=_cparams(1),
        name="proj",
    )(x, w)


def _proj_t_kernel(x_ref, wt_ref, o_ref, *, scale):
    y = lax.dot_general(wt_ref[...], x_ref[0].astype(BF16), _NT, preferred_element_type=F32)
    o_ref[0] = (y * scale).astype(o_ref.dtype)


def _proj_t(x, wt, *, scale, tm):
    bsz, t, d = x.shape
    n = wt.shape[0]
    return pl.pallas_call(
        functools.partial(_proj_t_kernel, scale=scale),
        grid=(bsz, t // tm),
        in_specs=[pl.BlockSpec((1, tm, d), lambda i, j: (i, j, 0)), _const_spec(wt.shape)],
        out_specs=pl.BlockSpec((1, n, tm), lambda i, j: (i, 0, j)),
        out_shape=jax.ShapeDtypeStruct((bsz, n, t), BF16),
        compiler_params=_cparams(2),
        name="proj_t",
    )(x, wt)


def _proj_ln_kernel(a_ref, x_ref, w_ref, g_ref, b_ref, o_ref, *, alpha):
    y = jnp.dot(a_ref[...].astype(BF16), w_ref[...], preferred_element_type=F32)
    o_ref[...] = _layer_norm(alpha * x_ref[...] + y, g_ref[...], b_ref[...])


def _proj_ln(a, x, w, g, b, *, alpha, tm):
    m, d = x.shape
    row = lambda i: (i, 0)
    return pl.pallas_call(
        functools.partial(_proj_ln_kernel, alpha=alpha),
        grid=(m // tm,),
        in_specs=[pl.BlockSpec((tm, a.shape[1]), row), pl.BlockSpec((tm, d), row),
                  _const_spec(w.shape), _const_spec(g.shape), _const_spec(b.shape)],
        out_specs=pl.BlockSpec((tm, d), row),
        out_shape=jax.ShapeDtypeStruct((m, d), F32),
        compiler_params=_cparams(1),
        name="proj_ln",
    )(a, x, w, g, b)


def _fa_kernel(qt_ref, k_ref, ak_ref, vt_ref, c_ref, o_ref, m_ref, l_ref, acc_ref, st_a_ref, st_b_ref,
               *, tq, dh):
    i = pl.program_id(1)
    hpt = LANES // dh
    n_pairs = k_ref.shape[2] // LANES
    sub = lax.broadcasted_iota(jnp.int32, (LANES, tq), 0)
    key_idx = lax.broadcasted_iota(jnp.int32, (tq, tq), 0)
    qry_idx = lax.broadcasted_iota(jnp.int32, (tq, tq), 1)

    def pair(g, carry):
        lo = pl.multiple_of(g * LANES, LANES)
        q2 = qt_ref[0, pl.ds(lo, LANES), :].astype(F32)
        c2 = c_ref[0, g]
        qp = []
        for jh in range(hpt):
            qm = jnp.where((sub >= jh * dh) & (sub < (jh + 1) * dh), q2, 0.0)
            hi, mid, lw = _split3(c2[jh:jh + 1, :])
            base = AUG_PER_HEAD * jh
            aug = jnp.where(sub == base, hi, jnp.where(sub == base + 1, mid, jnp.where(sub == base + 2, lw,
                            jnp.where((sub >= base + N_SPLIT) & (sub < base + AUG_PER_HEAD), 1.0, 0.0))))
            qp.append(jnp.concatenate([qm, aug], axis=0).astype(BF16))
        m_ref[...] = jnp.full_like(m_ref, -jnp.inf)
        l_ref[...] = jnp.zeros_like(l_ref)
        acc_ref[...] = jnp.zeros_like(acc_ref)

        def scores(kb, st_ref):
            ks = pl.multiple_of(kb * tq, tq)
            kk = jnp.concatenate([k_ref[0, pl.ds(ks, tq), pl.ds(lo, LANES)],
                                  ak_ref[0, pl.ds(ks, tq), pl.ds(lo, LANES)]], axis=1)
            for jh in range(hpt):
                st_ref[jh] = jnp.dot(kk, qp[jh], preferred_element_type=F32)

        def accumulate(kb, st_ref, masked):
            ks = pl.multiple_of(kb * tq, tq)
            v2 = vt_ref[0, pl.ds(lo, LANES), pl.ds(ks, tq)]
            for jh in range(hpt):
                st = st_ref[jh]
                if masked:
                    st = jnp.where(key_idx <= qry_idx, st, -jnp.inf)
                m_old = m_ref[jh]
                m_new = jnp.maximum(m_old, jnp.max(st, axis=0, keepdims=True))
                a = jnp.exp2(m_old - m_new)
                p = jnp.exp2(st - m_new)
                l_ref[jh] = a * l_ref[jh] + jnp.sum(p, axis=0, keepdims=True)
                pv = jnp.dot(v2[jh * dh:(jh + 1) * dh, :], p.astype(BF16), preferred_element_type=F32)
                acc_ref[jh] = a * acc_ref[jh] + pv
                m_ref[jh] = m_new

        def two_steps(tt, c):
            kb = 2 * tt
            scores(kb + 1, st_b_ref)
            accumulate(kb, st_a_ref, False)
            scores(kb + 2, st_a_ref)
            accumulate(kb + 1, st_b_ref, False)
            return c

        scores(0, st_a_ref)
        lax.fori_loop(0, i // 2, two_steps, 0)

        @pl.when(i % 2 == 1)
        def _():
            scores(i, st_b_ref)
            accumulate(i - 1, st_a_ref, False)
            accumulate(i, st_b_ref, True)

        @pl.when(i % 2 == 0)
        def _():
            accumulate(i, st_a_ref, True)
        ot = jnp.concatenate([acc_ref[jh] / l_ref[jh] for jh in range(hpt)], axis=0)
        o_ref[0, :, pl.ds(lo, LANES)] = ot.T.astype(o_ref.dtype)
        return carry

    lax.fori_loop(0, n_pairs, pair, 0)


def _flash_attn(qt, kb, ak, vtb, ct, *, nh, tq):
    bsz, d, t = qt.shape
    dh = d // nh
    hpt = LANES // dh
    n_pairs = nh // hpt
    res3 = lambda shape: pl.BlockSpec(shape, lambda b, i: (b, 0, 0), pipeline_mode=pl.Buffered(1))
    return pl.pallas_call(
        functools.partial(_fa_kernel, tq=tq, dh=dh),
        grid=(bsz, t // tq),
        in_specs=[
            pl.BlockSpec((1, d, tq), lambda b, i: (b, 0, i)),
            res3((1, t, d)), res3((1, t, d)), res3((1, d, t)),
            pl.BlockSpec((1, n_pairs, hpt, tq), lambda b, i: (b, 0, 0, i)),
        ],
        out_specs=pl.BlockSpec((1, tq, d), lambda b, i: (b, i, 0)),
        out_shape=jax.ShapeDtypeStruct((bsz, t, d), BF16),
        scratch_shapes=[pltpu.VMEM((hpt, 1, tq), F32), pltpu.VMEM((hpt, 1, tq), F32),
                        pltpu.VMEM((hpt, dh, tq), F32),
                        pltpu.VMEM((hpt, tq, tq), F32), pltpu.VMEM((hpt, tq, tq), F32)],
        compiler_params=_cparams(2),
        name="flash_attn",
    )(qt, kb, ak, vtb, ct.reshape(bsz, n_pairs, hpt, t))


def _decode_kernel(pt_ref, q_ref, kn_ref, vn_ref, lfn_ref, *refs, pages, nh, dh):
    del pt_ref
    kt_refs = refs[:pages]
    vt_refs = refs[pages:2 * pages]
    lf_refs = refs[2 * pages:3 * pages]
    o_ref = refs[3 * pages]
    qbd_ref, m_ref, l_ref, acc_ref, r_ref = refs[3 * pages + 1:]
    c = pl.program_id(1)
    d = nh * dh
    page = kt_refs[0].shape[2]
    head_row = lax.broadcasted_iota(jnp.int32, (nh, d), 0)
    head_of_lane = lax.broadcasted_iota(jnp.int32, (nh, d), 1) // dh
    block_diag = head_of_lane == head_row

    @pl.when(c == 0)
    def _():
        qbd = jnp.where(block_diag, jnp.broadcast_to(q_ref[0], (nh, d)), 0.0)
        qbd_ref[...] = qbd.astype(BF16)
        m_ref[...] = jnp.sum(qbd * kn_ref[0], axis=-1, keepdims=True)
        l_ref[...] = jnp.ones_like(l_ref)
        acc_ref[...] = jnp.broadcast_to(vn_ref[0], (nh, d))
        eye = (lax.broadcasted_iota(jnp.int32, (nh, nh), 0)
               == lax.broadcasted_iota(jnp.int32, (nh, nh), 1))
        lfn = jnp.broadcast_to(lfn_ref[0], (nh, nh))
        r_ref[...] = jnp.sum(jnp.where(eye, lfn, 0.0), axis=-1, keepdims=True)

    qbd = qbd_ref[...]
    r = r_ref[...]
    key_lane = lax.broadcasted_iota(jnp.int32, (nh, page), 1)
    scores = [None] * pages
    for pg in reversed(range(pages)):
        s = jnp.dot(qbd, kt_refs[pg][0].astype(BF16), preferred_element_type=F32)
        lft = lf_refs[pg][0]
        inc = lft
        shift = 1
        while shift < page:
            moved = pltpu.roll(inc, page - shift, 1)
            inc = inc + jnp.where(key_lane < page - shift, moved, 0.0)
            shift *= 2
        scores[pg] = s + ((inc - lft) + r)
        r = r + inc[:, 0:1]
    r_ref[...] = r
    s_all = jnp.concatenate(scores, axis=1)
    m_old = m_ref[...]
    m_new = jnp.maximum(m_old, jnp.max(s_all, axis=-1, keepdims=True))
    a = jnp.exp(m_old - m_new)
    p = jnp.exp(s_all - m_new)
    l_ref[...] = a * l_ref[...] + jnp.sum(p, axis=-1, keepdims=True)
    p = p.astype(BF16)
    pv = None
    for pg in range(pages):
        part = lax.dot_general(p[:, pg * page:(pg + 1) * page], vt_refs[pg][0].astype(BF16), _NT,
                               preferred_element_type=F32)
        pv = part if pv is None else pv + part
    acc_ref[...] = a * acc_ref[...] + pv
    m_ref[...] = m_new

    @pl.when(c == pl.num_programs(1) - 1)
    def _():
        o_ref[0] = jnp.sum(jnp.where(block_diag, acc_ref[...] / l_ref[...], 0.0), axis=0, keepdims=True)


def _decode_attn(page_table, q, k_new, v_new, lf_new, cache_kt, cache_vt, cache_lft, *, nh, pages):
    n, d = q.shape
    dh = d // nh
    page = cache_lft.shape[2]
    n_chunks = page_table.shape[1] // pages

    def tok_spec(width):
        return pl.BlockSpec((1, 1, width), lambda b, c, pt: (b, 0, 0))

    def page_spec(rows, pg):
        return pl.BlockSpec(
            (1, rows, page),
            lambda b, c, pt: (pt[b, (n_chunks - 1 - c) * pages + pg], 0, 0))

    in_specs = [tok_spec(d), tok_spec(d), tok_spec(d), tok_spec(nh)]
    in_specs += [page_spec(d, pg) for pg in range(pages)]
    in_specs += [page_spec(d, pg) for pg in range(pages)]
    in_specs += [page_spec(nh, pg) for pg in range(pages)]
    grid_spec = pltpu.PrefetchScalarGridSpec(
        num_scalar_prefetch=1,
        grid=(n, n_chunks),
        in_specs=in_specs,
        out_specs=tok_spec(d),
        scratch_shapes=[
            pltpu.VMEM((nh, d), BF16), pltpu.VMEM((nh, 1), F32), pltpu.VMEM((nh, 1), F32),
            pltpu.VMEM((nh, d), F32), pltpu.VMEM((nh, 1), F32),
        ],
    )
    out = pl.pallas_call(
        functools.partial(_decode_kernel, pages=pages, nh=nh, dh=dh),
        grid_spec=grid_spec,
        out_shape=jax.ShapeDtypeStruct((n, 1, d), F32),
        compiler_params=_cparams(2),
        name="decode_attn",
    )(page_table, q.reshape(n, 1, d), k_new.reshape(n, 1, d), v_new.reshape(n, 1, d),
      lf_new.reshape(n, 1, nh), *([cache_kt] * pages), *([cache_vt] * pages), *([cache_lft] * pages))
    return out.reshape(n, d)


def _bias_placement(nh, dh, d):
    hpt = LANES // dh
    place = np.zeros((LANES, d), np.float32)
    for h in range(nh):
        col0 = (h // hpt) * LANES + AUG_PER_HEAD * (h % hpt)
        for piece in range(N_SPLIT):
            place[N_SPLIT * nh, col0 + piece] = 1.0
            place[piece * nh + h, col0 + N_SPLIT + piece] = -1.0
    return place


def kernel(x_prompt, x_sample, cache_k, cache_v, cache_logf, state_conv, page_table, a_w_in, a_conv_w, a_w_out, b_w_q, b_w_o, kv_w, f_w, f_b, ffn_w_gu, ffn_w_down, ln_g, ln_b):
    bsz, t, d = x_prompt.shape
    n_s = x_sample.shape[0]
    n_a, n_b = a_w_in.shape[0], b_w_q.shape[0]
    depth = n_a + n_b
    nh = f_w.shape[1]
    dh = d // nh
    n_pool, page = cache_logf.shape[0], cache_logf.shape[1]
    alpha = (2 * depth) ** 0.25
    attn_scale = dh ** -0.5
    m = bsz * t
    tm_seq, tm_ffn, tq, dec_pages = 512, 256, 512, 8
    assert LANES % dh == 0 and (N_SPLIT + 1) * nh <= LANES and AUG_PER_HEAD * (LANES // dh) <= LANES

    w_in, w_out = a_w_in.astype(BF16), a_w_out.astype(BF16)
    w_q, w_o, w_kv = b_w_q.astype(BF16), b_w_o.astype(BF16), kv_w.astype(BF16)
    w_qt = jnp.swapaxes(b_w_q, 1, 2).astype(BF16)
    w_kvt = kv_w.T.astype(BF16)
    w_gu, w_down = ffn_w_gu.astype(BF16), ffn_w_down.astype(BF16)
    fw_pad = jnp.pad(f_w, ((0, 0), (0, LANES - nh))).astype(BF16)
    fb_pad = jnp.pad(f_b, (0, LANES - nh)).reshape(1, LANES)
    fw_rep = jnp.pad(jnp.tile(f_w, (1, N_SPLIT)), ((0, 0), (0, LANES - N_SPLIT * nh))).astype(BF16)
    fb_rep = jnp.pad(jnp.tile(f_b, N_SPLIT), (0, LANES - N_SPLIT * nh)).reshape(1, LANES)
    fwt = f_w.T.astype(BF16)
    fbt = f_b.reshape(nh, 1)
    place = jnp.asarray(_bias_placement(nh, dh, d), BF16)
    g = ln_g.reshape(depth, 2, 1, d)
    b = ln_b.reshape(depth, 2, 1, d)

    x = x_prompt
    hist0 = jnp.zeros((bsz, 2, d), F32)
    conv_p = []
    for l in range(n_a):
        x, st = _conv_seq(x, hist0, w_in[l], a_conv_w[l], w_out[l], g[l, 0], b[l, 0], alpha=alpha, tm=tm_seq)
        conv_p.append(st)
        x = _ffn(x.reshape(m, d), w_gu[l], w_down[l], g[l, 1], b[l, 1], alpha=alpha, tm=tm_ffn).reshape(bsz, t, d)
    kt_p, vt_p, lft_p, kb, ak, vtb, ct = _kv_seq(x, w_kvt, w_kv[:, :d], fw_rep, fb_rep, fwt, fbt, place,
                                                 nh=nh, tm=tm_seq)
    x = x.reshape(m, d)
    for j in range(n_b):
        l = n_a + j
        qt = _proj_t(x.reshape(bsz, t, d), w_qt[j], scale=attn_scale * LOG2E, tm=tm_seq)
        o = _flash_attn(qt, kb, ak, vtb, ct, nh=nh, tq=tq)
        x = _proj_ln(o.reshape(m, d), x, w_o[j], g[l, 0], b[l, 0], alpha=alpha, tm=tm_seq)
        x = _ffn(x, w_gu[l], w_down[l], g[l, 1], b[l, 1], alpha=alpha, tm=tm_ffn)
    y_prompt = x.reshape(bsz, t, d)

    cache_kt = jnp.transpose(cache_k, (0, 2, 3, 1)).reshape(n_pool, d, page)
    cache_vt = jnp.transpose(cache_v, (0, 2, 3, 1)).reshape(n_pool, d, page)
    cache_lft = jnp.transpose(cache_logf, (0, 2, 1))
    xs = x_sample.reshape(n_s, d)
    conv_s = []
    for l in range(n_a):
        xs, st = _conv_tok(xs, state_conv[l].reshape(n_s, 2 * d), w_in[l], a_conv_w[l], w_out[l],
                           g[l, 0], b[l, 0], alpha=alpha)
        conv_s.append(st.reshape(n_s, 2, d))
        xs = _ffn(xs, w_gu[l], w_down[l], g[l, 1], b[l, 1], alpha=alpha, tm=n_s)
    k_s, v_s, lf_s = _kv_tok(xs, w_kv, fw_pad, fb_pad, nh=nh)
    for j in range(n_b):
        l = n_a + j
        qs = _proj(xs, w_q[j], scale=attn_scale, out_dtype=F32, tm=n_s)
        os_ = _decode_attn(page_table, qs, k_s, v_s, lf_s, cache_kt, cache_vt, cache_lft, nh=nh, pages=dec_pages)
        xs = _proj_ln(os_, xs, w_o[j], g[l, 0], b[l, 0], alpha=alpha, tm=n_s)
        xs = _ffn(xs, w_gu[l], w_down[l], g[l, 1], b[l, 1], alpha=alpha, tm=n_s)

    return (y_prompt, xs.reshape(n_s, 1, d),
            jnp.transpose(kt_p, (0, 3, 1, 2)), jnp.transpose(vt_p, (0, 3, 1, 2)), jnp.transpose(lft_p, (0, 2, 1)),
            jnp.stack(conv_p),
            k_s.reshape(n_s, 1, nh, dh), v_s.reshape(n_s, 1, nh, dh), lf_s.reshape(n_s, 1, nh),
            jnp.stack(conv_s))
```

```python
import functools
import math

import numpy as np
import jax
import jax.numpy as jnp
from jax import lax
from jax.experimental import pallas as pl
from jax.experimental.pallas import tpu as pltpu

F32 = jnp.float32
BF16 = jnp.bfloat16
LN_EPS = 1e-5
LOG2E = math.log2(math.e)
V7X_VMEM_LIMIT_BYTES = 56 * 1024 * 1024
LANES = 128
N_SPLIT = 3
AUG_PER_HEAD = 2 * N_SPLIT
V_PAD = 16
FA_TILES = 2

_NT = (((1,), (1,)), ((), ()))


def _cparams(n_axes):
    return pltpu.CompilerParams(
        dimension_semantics=("arbitrary",) * n_axes,
        vmem_limit_bytes=V7X_VMEM_LIMIT_BYTES,
    )


def _const_spec(shape):
    nd = len(shape)
    return pl.BlockSpec(shape, lambda *_: (0,) * nd, pipeline_mode=pl.Buffered(1))


def _layer_norm(r, g, b):
    mu = jnp.mean(r, axis=-1, keepdims=True)
    c = r - mu
    var = jnp.mean(c * c, axis=-1, keepdims=True)
    return c * lax.rsqrt(var + LN_EPS) * g + b


def _log_sigmoid(z):
    return jnp.minimum(z, 0.0) - jnp.log1p(jnp.exp(-jnp.abs(z)))


def _split3(c):
    hi = c.astype(BF16).astype(F32)
    r = c - hi
    mid = r.astype(BF16).astype(F32)
    lo = (r - mid).astype(BF16).astype(F32)
    return hi, mid, lo


def _gated_conv_out(x, bch, p1, p2, cw_ref, w_out_ref, g_ref, b_ref, alpha):
    d = x.shape[-1]
    u = bch[:, d:2 * d] * bch[:, 2 * d:]
    z = cw_ref[2:3, :] * u
    z = z + cw_ref[0:1, :] * p2
    z = z + cw_ref[1:2, :] * p1
    y = jnp.dot((bch[:, :d] * z).astype(BF16), w_out_ref[...], preferred_element_type=F32)
    return _layer_norm(alpha * x + y, g_ref[...], b_ref[...])


def _conv_seq_kernel(x_ref, hist_ref, w_in_ref, cw_ref, w_out_ref, g_ref, b_ref,
                     o_ref, st_ref, carry_ref, *, alpha):
    j = pl.program_id(1)

    @pl.when(j == 0)
    def _():
        carry_ref[...] = hist_ref[0]

    x = x_ref[0]
    tm, d = x.shape
    bch = jnp.dot(x.astype(BF16), w_in_ref[...], preferred_element_type=F32)
    u = bch[:, d:2 * d] * bch[:, 2 * d:]
    h0 = carry_ref[0:1, :]
    h1 = carry_ref[1:2, :]
    row = lax.broadcasted_iota(jnp.int32, (tm, d), 0)
    p1 = jnp.where(row == 0, h1, pltpu.roll(u, 1, 0))
    p2 = jnp.where(row == 0, h0, jnp.where(row == 1, h1, pltpu.roll(u, 2, 0)))
    tail = u[tm - 2:, :]
    carry_ref[...] = tail
    o_ref[0] = _gated_conv_out(x, bch, p1, p2, cw_ref, w_out_ref, g_ref, b_ref, alpha)

    @pl.when(j == pl.num_programs(1) - 1)
    def _():
        st_ref[0] = tail


def _conv_seq(x, hist, w_in, cw, w_out, g, b, *, alpha, tm):
    bsz, t, d = x.shape
    return pl.pallas_call(
        functools.partial(_conv_seq_kernel, alpha=alpha),
        grid=(bsz, t // tm),
        in_specs=[
            pl.BlockSpec((1, tm, d), lambda i, j: (i, j, 0)),
            pl.BlockSpec((1, 2, d), lambda i, j: (i, 0, 0)),
            _const_spec(w_in.shape), _const_spec(cw.shape), _const_spec(w_out.shape),
            _const_spec(g.shape), _const_spec(b.shape),
        ],
        out_specs=[
            pl.BlockSpec((1, tm, d), lambda i, j: (i, j, 0)),
            pl.BlockSpec((1, 2, d), lambda i, j: (i, 0, 0)),
        ],
        out_shape=[jax.ShapeDtypeStruct((bsz, t, d), F32), jax.ShapeDtypeStruct((bsz, 2, d), F32)],
        scratch_shapes=[pltpu.VMEM((2, d), F32)],
        compiler_params=_cparams(2),
        name="conv_seq",
    )(x, hist, w_in, cw, w_out, g, b)


def _conv_tok_kernel(x_ref, hist_ref, w_in_ref, cw_ref, w_out_ref, g_ref, b_ref,
                     o_ref, st_ref, *, alpha):
    x = x_ref[...]
    d = x.shape[-1]
    bch = jnp.dot(x.astype(BF16), w_in_ref[...], preferred_element_type=F32)
    u = bch[:, d:2 * d] * bch[:, 2 * d:]
    h0 = hist_ref[:, :d]
    h1 = hist_ref[:, d:]
    o_ref[...] = _gated_conv_out(x, bch, h1, h0, cw_ref, w_out_ref, g_ref, b_ref, alpha)
    st_ref[:, :d] = h1
    st_ref[:, d:] = u


def _conv_tok(x, hist, w_in, cw, w_out, g, b, *, alpha):
    n, d = x.shape
    args = (x, hist, w_in, cw, w_out, g, b)
    return pl.pallas_call(
        functools.partial(_conv_tok_kernel, alpha=alpha),
        grid=(1,),
        in_specs=[_const_spec(a.shape) for a in args],
        out_specs=[_const_spec((n, d)), _const_spec((n, 2 * d))],
        out_shape=[jax.ShapeDtypeStruct((n, d), F32), jax.ShapeDtypeStruct((n, 2 * d), F32)],
        compiler_params=_cparams(1),
        name="conv_tok",
    )(*args)


def _ffn_kernel(x_ref, wgu_ref, wd_ref, g_ref, b_ref, o_ref, *, alpha):
    x = x_ref[...]
    dff = wd_ref.shape[0]
    gu = jnp.dot(x.astype(BF16), wgu_ref[...], preferred_element_type=F32)
    gate = gu[:, :dff]
    a = (gate * jax.nn.sigmoid(gate)) * gu[:, dff:]
    y = jnp.dot(a.astype(BF16), wd_ref[...], preferred_element_type=F32)
    o_ref[...] = _layer_norm(alpha * x + y, g_ref[...], b_ref[...])


def _ffn(x, wgu, wd, g, b, *, alpha, tm):
    m, d = x.shape
    return pl.pallas_call(
        functools.partial(_ffn_kernel, alpha=alpha),
        grid=(m // tm,),
        in_specs=[
            pl.BlockSpec((tm, d), lambda i: (i, 0)),
            _const_spec(wgu.shape), _const_spec(wd.shape), _const_spec(g.shape), _const_spec(b.shape),
        ],
        out_specs=pl.BlockSpec((tm, d), lambda i: (i, 0)),
        out_shape=jax.ShapeDtypeStruct((m, d), F32),
        compiler_params=_cparams(1),
        name="ffn",
    )(x, wgu, wd, g, b)


def _kv_seq_kernel(x_ref, wkvt_ref, wk_ref, fw_ref, fb_ref, fwt_ref, fbt_ref, place_ref,
                   kt_ref, vt_ref, lft_ref, kb_ref, ak_ref, vtb_ref, ct_ref,
                   carry_ref, carry_t_ref):
    j = pl.program_id(1)

    @pl.when(j == 0)
    def _():
        carry_ref[...] = jnp.zeros_like(carry_ref)
        carry_t_ref[...] = jnp.zeros_like(carry_t_ref)

    xb = x_ref[0].astype(BF16)
    tm, d = xb.shape
    nh = lft_ref.shape[1]
    kvt = lax.dot_general(wkvt_ref[...], xb, _NT, preferred_element_type=F32)
    kt_ref[0] = kvt[:d].reshape(kt_ref.shape[1:])
    vt3 = kvt[d:].reshape(vt_ref.shape[1:])
    vt_ref[0] = vt3
    ones = jnp.ones((nh, V_PAD, tm), F32)
    vtb_ref[0] = jnp.concatenate([vt3, ones], axis=1).reshape(vtb_ref.shape[1:]).astype(BF16)
    kb_ref[0] = jnp.dot(xb, wk_ref[...], preferred_element_type=F32).astype(BF16)

    lft = _log_sigmoid(lax.dot_general(fwt_ref[...], xb, _NT, preferred_element_type=F32) + fbt_ref[...])
    lft_ref[0] = lft
    lane = lax.broadcasted_iota(jnp.int32, lft.shape, 1)
    ct = lft
    shift = 1
    while shift < tm:
        ct = ct + jnp.where(lane >= shift, pltpu.roll(ct, shift, 1), 0.0)
        shift *= 2
    ct = ct + carry_t_ref[...]
    carry_t_ref[...] = ct[:, tm - 1:]
    ct_ref[0] = ct * LOG2E

    lf = _log_sigmoid(jnp.dot(xb, fw_ref[...], preferred_element_type=F32) + fb_ref[...])
    row = lax.broadcasted_iota(jnp.int32, lf.shape, 0)
    c = lf
    shift = 1
    while shift < tm:
        c = c + jnp.where(row >= shift, pltpu.roll(c, shift, 0), 0.0)
        shift *= 2
    c = c + carry_ref[...]
    carry_ref[...] = c[tm - 1:, :]
    hi, mid, lo = _split3(c * LOG2E)
    grp = lax.broadcasted_iota(jnp.int32, lf.shape, 1) // nh
    pieces = jnp.where(grp == 0, hi, jnp.where(grp == 1, mid, jnp.where(grp == 2, lo,
                       jnp.where(grp == N_SPLIT, 1.0, 0.0))))
    ak_ref[0] = jnp.dot(pieces.astype(BF16), place_ref[...], preferred_element_type=F32).astype(BF16)


def _kv_seq(x, wkvt, wk, fw, fb, fwt, fbt, place, *, nh, tm):
    bsz, t, d = x.shape
    dh = d // nh
    vrows = nh * (dh + V_PAD)
    tok = pl.BlockSpec((1, tm, d), lambda i, j: (i, j, 0))
    tr4 = pl.BlockSpec((1, nh, dh, tm), lambda i, j: (i, 0, 0, j))
    tr3 = pl.BlockSpec((1, nh, tm), lambda i, j: (i, 0, j))
    trv = pl.BlockSpec((1, vrows, tm), lambda i, j: (i, 0, j))
    consts = (wkvt, wk, fw, fb, fwt, fbt, place)
    return pl.pallas_call(
        _kv_seq_kernel,
        grid=(bsz, t // tm),
        in_specs=[tok] + [_const_spec(a.shape) for a in consts],
        out_specs=[tr4, tr4, tr3, tok, tok, trv, tr3],
        out_shape=[
            jax.ShapeDtypeStruct((bsz, nh, dh, t), F32), jax.ShapeDtypeStruct((bsz, nh, dh, t), F32),
            jax.ShapeDtypeStruct((bsz, nh, t), F32),
            jax.ShapeDtypeStruct((bsz, t, d), BF16), jax.ShapeDtypeStruct((bsz, t, d), BF16),
            jax.ShapeDtypeStruct((bsz, vrows, t), BF16), jax.ShapeDtypeStruct((bsz, nh, t), F32),
        ],
        scratch_shapes=[pltpu.VMEM((1, fw.shape[1]), F32), pltpu.VMEM((nh, 1), F32)],
        compiler_params=_cparams(2),
        name="kv_seq",
    )(x, *consts)


def _kv_tok_kernel(x_ref, wkv_ref, fw_ref, fb_ref, k_ref, v_ref, lf_ref):
    xb = x_ref[...].astype(BF16)
    d = xb.shape[1]
    nh = lf_ref.shape[-1]
    kv = jnp.dot(xb, wkv_ref[...], preferred_element_type=F32)
    k_ref[...] = kv[:, :d]
    v_ref[...] = kv[:, d:]
    lf = _log_sigmoid(jnp.dot(xb, fw_ref[...], preferred_element_type=F32) + fb_ref[...])
    lf_ref[...] = lf[:, :nh]


def _kv_tok(x, wkv, fw, fb, *, nh):
    n, d = x.shape
    args = (x, wkv, fw, fb)
    return pl.pallas_call(
        _kv_tok_kernel,
        grid=(1,),
        in_specs=[_const_spec(a.shape) for a in args],
        out_specs=[_const_spec((n, d)), _const_spec((n, d)), _const_spec((n, nh))],
        out_shape=[jax.ShapeDtypeStruct((n, d), F32), jax.ShapeDtypeStruct((n, d), F32),
                   jax.ShapeDtypeStruct((n, nh), F32)],
        compiler_params=_cparams(1),
        name="kv_tok",
    )(*args)


def _proj_kernel(x_ref, w_ref, o_ref, *, scale):
    y = jnp.dot(x_ref[...].astype(BF16), w_ref[...], preferred_element_type=F32)
    o_ref[...] = (y * scale).astype(o_ref.dtype)


def _proj(x, w, *, scale, out_dtype, tm):
    m, d = x.shape
    n = w.shape[1]
    return pl.pallas_call(
        functools.partial(_proj_kernel, scale=scale),
        grid=(m // tm,),
        in_specs=[pl.BlockSpec((tm, d), lambda i: (i, 0)), _const_spec(w.shape)],
        out_specs=pl.BlockSpec((tm, n), lambda i: (i, 0)),
        out_shape=jax.ShapeDtypeStruct((m, n), out_dtype),
        compiler_params=_cparams(1),
        name="proj",
    )(x, w)


def _proj_t_kernel(x_ref, wt_ref, o_ref, *, scale):
    y = lax.dot_general(wt_ref[...], x_ref[0].astype(BF16), _NT, preferred_element_type=F32)
    o_ref[0] = (y * scale).astype(o_ref.dtype)


def _proj_t(x, wt, *, scale, tm):
    bsz, t, d = x.shape
    n = wt.shape[0]
    return pl.pallas_call(
        functools.partial(_proj_t_kernel, scale=scale),
        grid=(bsz, t // tm),
        in_specs=[pl.BlockSpec((1, tm, d), lambda i, j: (i, j, 0)), _const_spec(wt.shape)],
        out_specs=pl.BlockSpec((1, n, tm), lambda i, j: (i, 0, j)),
        out_shape=jax.ShapeDtypeStruct((bsz, n, t), BF16),
        compiler_params=_cparams(2),
        name="proj_t",
    )(x, wt)


def _proj_ln_kernel(a_ref, x_ref, w_ref, g_ref, b_ref, o_ref, *, alpha):
    y = jnp.dot(a_ref[...].astype(BF16), w_ref[...], preferred_element_type=F32)
    o_ref[...] = _layer_norm(alpha * x_ref[...] + y, g_ref[...], b_ref[...])


def _proj_ln(a, x, w, g, b, *, alpha, tm):
    m, d = x.shape
    row = lambda i: (i, 0)
    return pl.pallas_call(
        functools.partial(_proj_ln_kernel, alpha=alpha),
        grid=(m // tm,),
        in_specs=[pl.BlockSpec((tm, a.shape[1]), row), pl.BlockSpec((tm, d), row),
                  _const_spec(w.shape), _const_spec(g.shape), _const_spec(b.shape)],
        out_specs=pl.BlockSpec((tm, d), row),
        out_shape=jax.ShapeDtypeStruct((m, d), F32),
        compiler_params=_cparams(1),
        name="proj_ln",
    )(a, x, w, g, b)


def _fa_kernel(qt_ref, k_ref, ak_ref, vt_ref, c_ref, o_ref, m_ref, acc_ref, st_a_ref, st_b_ref, *, tq, dh):
    i = pl.program_id(1)
    hpt = LANES // dh
    vrows = dh + V_PAD
    n_groups = k_ref.shape[2] // (LANES * FA_TILES)
    sub = lax.broadcasted_iota(jnp.int32, (LANES, tq), 0)
    key_idx = lax.broadcasted_iota(jnp.int32, (tq, tq), 0)
    qry_idx = lax.broadcasted_iota(jnp.int32, (tq, tq), 1)

    def group(gi, carry):
        tiles = [gi * FA_TILES + lt for lt in range(FA_TILES)]
        los = [pl.multiple_of(tile * LANES, LANES) for tile in tiles]
        qp = []
        for tile, lo in zip(tiles, los):
            q2 = qt_ref[0, pl.ds(lo, LANES), :].astype(F32)
            c2 = c_ref[0, tile]
            for jh in range(hpt):
                qm = jnp.where((sub >= jh * dh) & (sub < (jh + 1) * dh), q2, 0.0)
                hi, mid, lw = _split3(c2[jh:jh + 1, :])
                base = AUG_PER_HEAD * jh
                aug = jnp.where(sub == base, hi, jnp.where(sub == base + 1, mid, jnp.where(sub == base + 2, lw,
                                jnp.where((sub >= base + N_SPLIT) & (sub < base + AUG_PER_HEAD), 1.0, 0.0))))
                qp.append(jnp.concatenate([qm, aug], axis=0).astype(BF16))
        m_ref[...] = jnp.full_like(m_ref, -jnp.inf)
        acc_ref[...] = jnp.zeros_like(acc_ref)

        def scores(kb, st_ref):
            ks = pl.multiple_of(kb * tq, tq)
            for lt, lo in enumerate(los):
                kk = jnp.concatenate([k_ref[0, pl.ds(ks, tq), pl.ds(lo, LANES)],
                                      ak_ref[0, pl.ds(ks, tq), pl.ds(lo, LANES)]], axis=1)
                for jh in range(hpt):
                    st_ref[lt * hpt + jh] = jnp.dot(kk, qp[lt * hpt + jh], preferred_element_type=F32)

        def accumulate(kb, st_ref, masked):
            ks = pl.multiple_of(kb * tq, tq)
            for lt, tile in enumerate(tiles):
                v2 = vt_ref[0, pl.ds(pl.multiple_of(tile * (hpt * vrows), hpt * vrows), hpt * vrows), pl.ds(ks, tq)]
                for jh in range(hpt):
                    hh = lt * hpt + jh
                    st = st_ref[hh]
                    if masked:
                        st = jnp.where(key_idx <= qry_idx, st, -jnp.inf)
                    m_old = m_ref[hh]
                    m_new = jnp.maximum(m_old, jnp.max(st, axis=0, keepdims=True))
                    p = jnp.exp2(st - m_new).astype(BF16)
                    pv = jnp.dot(v2[jh * vrows:(jh + 1) * vrows, :], p, preferred_element_type=F32)
                    acc_ref[hh] = jnp.exp2(m_old - m_new) * acc_ref[hh] + pv
                    m_ref[hh] = m_new

        def two_steps(tt, c):
            kb = 2 * tt
            scores(kb + 1, st_b_ref)
            accumulate(kb, st_a_ref, False)
            scores(kb + 2, st_a_ref)
            accumulate(kb + 1, st_b_ref, False)
            return c

        scores(0, st_a_ref)
        lax.fori_loop(0, i // 2, two_steps, 0)

        @pl.when(i % 2 == 1)
        def _():
            scores(i, st_b_ref)
            accumulate(i - 1, st_a_ref, False)
            accumulate(i, st_b_ref, True)

        @pl.when(i % 2 == 0)
        def _():
            accumulate(i, st_a_ref, True)

        for lt, lo in enumerate(los):
            heads = [acc_ref[lt * hpt + jh] for jh in range(hpt)]
            ot = jnp.concatenate([acc[:dh] / acc[dh:dh + 1] for acc in heads], axis=0)
            o_ref[0, :, pl.ds(lo, LANES)] = ot.T.astype(o_ref.dtype)
        return carry

    lax.fori_loop(0, n_groups, group, 0)


def _flash_attn(qt, kb, ak, vtb, ct, *, nh, tq):
    bsz, d, t = qt.shape
    dh = d // nh
    hpt = LANES // dh
    n_tiles = nh // hpt
    heads_per_step = FA_TILES * hpt
    assert n_tiles % FA_TILES == 0
    res3 = lambda shape: pl.BlockSpec(shape, lambda b, i: (b, 0, 0), pipeline_mode=pl.Buffered(1))
    return pl.pallas_call(
        functools.partial(_fa_kernel, tq=tq, dh=dh),
        grid=(bsz, t // tq),
        in_specs=[
            pl.BlockSpec((1, d, tq), lambda b, i: (b, 0, i)),
            res3((1, t, d)), res3((1, t, d)), res3((1, vtb.shape[1], t)),
            pl.BlockSpec((1, n_tiles, hpt, tq), lambda b, i: (b, 0, 0, i)),
        ],
        out_specs=pl.BlockSpec((1, tq, d), lambda b, i: (b, i, 0)),
        out_shape=jax.ShapeDtypeStruct((bsz, t, d), BF16),
        scratch_shapes=[pltpu.VMEM((heads_per_step, 1, tq), F32),
                        pltpu.VMEM((heads_per_step, dh + V_PAD, tq), F32),
                        pltpu.VMEM((heads_per_step, tq, tq), F32), pltpu.VMEM((heads_per_step, tq, tq), F32)],
        compiler_params=_cparams(2),
        name="flash_attn",
    )(qt, kb, ak, vtb, ct.reshape(bsz, n_tiles, hpt, t))


def _decode_kernel(pt_ref, q_ref, kn_ref, vn_ref, lfn_ref, *refs, pages, nh, dh):
    del pt_ref
    kt_refs = refs[:pages]
    vt_refs = refs[pages:2 * pages]
    lf_refs = refs[2 * pages:3 * pages]
    o_ref = refs[3 * pages]
    qbd_ref, m_ref, l_ref, acc_ref, r_ref = refs[3 * pages + 1:]
    c = pl.program_id(1)
    d = nh * dh
    page = kt_refs[0].shape[2]
    head_row = lax.broadcasted_iota(jnp.int32, (nh, d), 0)
    head_of_lane = lax.broadcasted_iota(jnp.int32, (nh, d), 1) // dh
    block_diag = head_of_lane == head_row

    @pl.when(c == 0)
    def _():
        qbd = jnp.where(block_diag, jnp.broadcast_to(q_ref[0], (nh, d)), 0.0)
        qbd_ref[...] = qbd.astype(BF16)
        m_ref[...] = jnp.sum(qbd * kn_ref[0], axis=-1, keepdims=True)
        l_ref[...] = jnp.ones_like(l_ref)
        acc_ref[...] = jnp.broadcast_to(vn_ref[0], (nh, d))
        eye = (lax.broadcasted_iota(jnp.int32, (nh, nh), 0)
               == lax.broadcasted_iota(jnp.int32, (nh, nh), 1))
        lfn = jnp.broadcast_to(lfn_ref[0], (nh, nh))
        r_ref[...] = jnp.sum(jnp.where(eye, lfn, 0.0), axis=-1, keepdims=True)

    qbd = qbd_ref[...]
    r = r_ref[...]
    key_lane = lax.broadcasted_iota(jnp.int32, (nh, page), 1)
    scores = [None] * pages
    for pg in reversed(range(pages)):
        s = jnp.dot(qbd, kt_refs[pg][0].astype(BF16), preferred_element_type=F32)
        lft = lf_refs[pg][0]
        inc = lft
        shift = 1
        while shift < page:
            moved = pltpu.roll(inc, page - shift, 1)
            inc = inc + jnp.where(key_lane < page - shift, moved, 0.0)
            shift *= 2
        scores[pg] = s + ((inc - lft) + r)
        r = r + inc[:, 0:1]
    r_ref[...] = r
    s_all = jnp.concatenate(scores, axis=1)
    m_old = m_ref[...]
    m_new = jnp.maximum(m_old, jnp.max(s_all, axis=-1, keepdims=True))
    a = jnp.exp(m_old - m_new)
    p = jnp.exp(s_all - m_new)
    l_ref[...] = a * l_ref[...] + jnp.sum(p, axis=-1, keepdims=True)
    p = p.astype(BF16)
    pv = None
    for pg in range(pages):
        part = lax.dot_general(p[:, pg * page:(pg + 1) * page], vt_refs[pg][0].astype(BF16), _NT,
                               preferred_element_type=F32)
        pv = part if pv is None else pv + part
    acc_ref[...] = a * acc_ref[...] + pv
    m_ref[...] = m_new

    @pl.when(c == pl.num_programs(1) - 1)
    def _():
        o_ref[0] = jnp.sum(jnp.where(block_diag, acc_ref[...] / l_ref[...], 0.0), axis=0, keepdims=True)


def _decode_attn(page_table, q, k_new, v_new, lf_new, cache_kt, cache_vt, cache_lft, *, nh, pages):
    n, d = q.shape
    dh = d // nh
    page = cache_lft.shape[2]
    n_chunks = page_table.shape[1] // pages

    def tok_spec(width):
        return pl.BlockSpec((1, 1, width), lambda b, c, pt: (b, 0, 0))

    def page_spec(rows, pg):
        return pl.BlockSpec(
            (1, rows, page),
            lambda b, c, pt: (pt[b, (n_chunks - 1 - c) * pages + pg], 0, 0))

    in_specs = [tok_spec(d), tok_spec(d), tok_spec(d), tok_spec(nh)]
    in_specs += [page_spec(d, pg) for pg in range(pages)]
    in_specs += [page_spec(d, pg) for pg in range(pages)]
    in_specs += [page_spec(nh, pg) for pg in range(pages)]
    grid_spec = pltpu.PrefetchScalarGridSpec(
        num_scalar_prefetch=1,
        grid=(n, n_chunks),
        in_specs=in_specs,
        out_specs=tok_spec(d),
        scratch_shapes=[
            pltpu.VMEM((nh, d), BF16), pltpu.VMEM((nh, 1), F32), pltpu.VMEM((nh, 1), F32),
            pltpu.VMEM((nh, d), F32), pltpu.VMEM((nh, 1), F32),
        ],
    )
    out = pl.pallas_call(
        functools.partial(_decode_kernel, pages=pages, nh=nh, dh=dh),
        grid_spec=grid_spec,
        out_shape=jax.ShapeDtypeStruct((n, 1, d), F32),
        compiler_params=_cparams(2),
        name="decode_attn",
    )(page_table, q.reshape(n, 1, d), k_new.reshape(n, 1, d), v_new.reshape(n, 1, d),
      lf_new.reshape(n, 1, nh), *([cache_kt] * pages), *([cache_vt] * pages), *([cache_lft] * pages))
    return out.reshape(n, d)


def _bias_placement(nh, dh, d):
    hpt = LANES // dh
    place = np.zeros((LANES, d), np.float32)
    for h in range(nh):
        col0 = (h // hpt) * LANES + AUG_PER_HEAD * (h % hpt)
        for piece in range(N_SPLIT):
            place[N_SPLIT * nh, col0 + piece] = 1.0
            place[piece * nh + h, col0 + N_SPLIT + piece] = -1.0
    return place


def kernel(x_prompt, x_sample, cache_k, cache_v, cache_logf, state_conv, page_table, a_w_in, a_conv_w, a_w_out, b_w_q, b_w_o, kv_w, f_w, f_b, ffn_w_gu, ffn_w_down, ln_g, ln_b):
    bsz, t, d = x_prompt.shape
    n_s = x_sample.shape[0]
    n_a, n_b = a_w_in.shape[0], b_w_q.shape[0]
    depth = n_a + n_b
    nh = f_w.shape[1]
    dh = d // nh
    n_pool, page = cache_logf.shape[0], cache_logf.shape[1]
    alpha = (2 * depth) ** 0.25
    attn_scale = dh ** -0.5
    m = bsz * t
    tm_seq, tm_ffn, tq, dec_pages = 512, 256, 512, 16
    assert LANES % dh == 0 and (N_SPLIT + 1) * nh <= LANES and AUG_PER_HEAD * (LANES // dh) <= LANES

    w_in, w_out = a_w_in.astype(BF16), a_w_out.astype(BF16)
    w_q, w_o, w_kv = b_w_q.astype(BF16), b_w_o.astype(BF16), kv_w.astype(BF16)
    w_qt = jnp.swapaxes(b_w_q, 1, 2).astype(BF16)
    w_kvt = kv_w.T.astype(BF16)
    w_gu, w_down = ffn_w_gu.astype(BF16), ffn_w_down.astype(BF16)
    fw_pad = jnp.pad(f_w, ((0, 0), (0, LANES - nh))).astype(BF16)
    fb_pad = jnp.pad(f_b, (0, LANES - nh)).reshape(1, LANES)
    fw_rep = jnp.pad(jnp.tile(f_w, (1, N_SPLIT)), ((0, 0), (0, LANES - N_SPLIT * nh))).astype(BF16)
    fb_rep = jnp.pad(jnp.tile(f_b, N_SPLIT), (0, LANES - N_SPLIT * nh)).reshape(1, LANES)
    fwt = f_w.T.astype(BF16)
    fbt = f_b.reshape(nh, 1)
    place = jnp.asarray(_bias_placement(nh, dh, d), BF16)
    g = ln_g.reshape(depth, 2, 1, d)
    b = ln_b.reshape(depth, 2, 1, d)

    x = x_prompt
    hist0 = jnp.zeros((bsz, 2, d), F32)
    conv_p = []
    for l in range(n_a):
        x, st = _conv_seq(x, hist0, w_in[l], a_conv_w[l], w_out[l], g[l, 0], b[l, 0], alpha=alpha, tm=tm_seq)
        conv_p.append(st)
        x = _ffn(x.reshape(m, d), w_gu[l], w_down[l], g[l, 1], b[l, 1], alpha=alpha, tm=tm_ffn).reshape(bsz, t, d)
    kt_p, vt_p, lft_p, kb, ak, vtb, ct = _kv_seq(x, w_kvt, w_kv[:, :d], fw_rep, fb_rep, fwt, fbt, place,
                                                 nh=nh, tm=tm_seq)
    x = x.reshape(m, d)
    for j in range(n_b):
        l = n_a + j
        qt = _proj_t(x.reshape(bsz, t, d), w_qt[j], scale=attn_scale * LOG2E, tm=tm_seq)
        o = _flash_attn(qt, kb, ak, vtb, ct, nh=nh, tq=tq)
        x = _proj_ln(o.reshape(m, d), x, w_o[j], g[l, 0], b[l, 0], alpha=alpha, tm=tm_seq)
        x = _ffn(x, w_gu[l], w_down[l], g[l, 1], b[l, 1], alpha=alpha, tm=tm_ffn)
    y_prompt = x.reshape(bsz, t, d)

    cache_kt = jnp.transpose(cache_k, (0, 2, 3, 1)).reshape(n_pool, d, page)
    cache_vt = jnp.transpose(cache_v, (0, 2, 3, 1)).reshape(n_pool, d, page)
    cache_lft = jnp.transpose(cache_logf, (0, 2, 1))
    xs = x_sample.reshape(n_s, d)
    conv_s = []
    for l in range(n_a):
        xs, st = _conv_tok(xs, state_conv[l].reshape(n_s, 2 * d), w_in[l], a_conv_w[l], w_out[l],
                           g[l, 0], b[l, 0], alpha=alpha)
        conv_s.append(st.reshape(n_s, 2, d))
        xs = _ffn(xs, w_gu[l], w_down[l], g[l, 1], b[l, 1], alpha=alpha, tm=n_s)
    k_s, v_s, lf_s = _kv_tok(xs, w_kv, fw_pad, fb_pad, nh=nh)
    for j in range(n_b):
        l = n_a + j
        qs = _proj(xs, w_q[j], scale=attn_scale, out_dtype=F32, tm=n_s)
        os_ = _decode_attn(page_table, qs, k_s, v_s, lf_s, cache_kt, cache_vt, cache_lft, nh=nh, pages=dec_pages)
        xs = _proj_ln(os_, xs, w_o[j], g[l, 0], b[l, 0], alpha=alpha, tm=n_s)
        xs = _ffn(xs, w_gu[l], w_down[l], g[l, 1], b[l, 1], alpha=alpha, tm=n_s)

    return (y_prompt, xs.reshape(n_s, 1, d),
            jnp.transpose(kt_p, (0, 3, 1, 2)), jnp.transpose(vt_p, (0, 3, 1, 2)), jnp.transpose(lft_p, (0, 2, 1)),
            jnp.stack(conv_p),
            k_s.reshape(n_s, 1, nh, dh), v_s.reshape(n_s, 1, nh, dh), lf_s.reshape(n_s, 1, nh),
            jnp.stack(conv_s))
```

```python
import functools
import math

import numpy as np
import jax
import jax.numpy as jnp
from jax import lax
from jax.experimental import pallas as pl
from jax.experimental.pallas import tpu as pltpu

F32 = jnp.float32
BF16 = jnp.bfloat16
LN_EPS = 1e-5
LOG2E = math.log2(math.e)
V7X_VMEM_LIMIT_BYTES = 56 * 1024 * 1024
LANES = 128
N_SPLIT = 3
AUG_PER_HEAD = 2 * N_SPLIT
V_PAD = 16
FA_TILES = 2

_NT = (((1,), (1,)), ((), ()))


def _cparams(n_axes):
    return pltpu.CompilerParams(
        dimension_semantics=("arbitrary",) * n_axes,
        vmem_limit_bytes=V7X_VMEM_LIMIT_BYTES,
    )


def _const_spec(shape):
    nd = len(shape)
    return pl.BlockSpec(shape, lambda *_: (0,) * nd, pipeline_mode=pl.Buffered(1))


def _layer_norm(r, g, b):
    mu = jnp.mean(r, axis=-1, keepdims=True)
    c = r - mu
    var = jnp.mean(c * c, axis=-1, keepdims=True)
    return c * lax.rsqrt(var + LN_EPS) * g + b


def _log_sigmoid(z):
    return jnp.minimum(z, 0.0) - jnp.log1p(jnp.exp(-jnp.abs(z)))


def _split3(c):
    hi = c.astype(BF16).astype(F32)
    r = c - hi
    mid = r.astype(BF16).astype(F32)
    lo = (r - mid).astype(BF16).astype(F32)
    return hi, mid, lo


def _gated_conv_out(x, bch, p1, p2, cw_ref, w_out_ref, g_ref, b_ref, alpha):
    d = x.shape[-1]
    u = bch[:, d:2 * d] * bch[:, 2 * d:]
    z = cw_ref[2:3, :] * u
    z = z + cw_ref[0:1, :] * p2
    z = z + cw_ref[1:2, :] * p1
    y = jnp.dot((bch[:, :d] * z).astype(BF16), w_out_ref[...], preferred_element_type=F32)
    return _layer_norm(alpha * x + y, g_ref[...], b_ref[...])


def _conv_seq_kernel(x_ref, hist_ref, w_in_ref, cw_ref, w_out_ref, g_ref, b_ref,
                     o_ref, st_ref, carry_ref, *, alpha):
    j = pl.program_id(1)

    @pl.when(j == 0)
    def _():
        carry_ref[...] = hist_ref[0]

    x = x_ref[0]
    tm, d = x.shape
    bch = jnp.dot(x.astype(BF16), w_in_ref[...], preferred_element_type=F32)
    u = bch[:, d:2 * d] * bch[:, 2 * d:]
    h0 = carry_ref[0:1, :]
    h1 = carry_ref[1:2, :]
    row = lax.broadcasted_iota(jnp.int32, (tm, d), 0)
    p1 = jnp.where(row == 0, h1, pltpu.roll(u, 1, 0))
    p2 = jnp.where(row == 0, h0, jnp.where(row == 1, h1, pltpu.roll(u, 2, 0)))
    tail = u[tm - 2:, :]
    carry_ref[...] = tail
    o_ref[0] = _gated_conv_out(x, bch, p1, p2, cw_ref, w_out_ref, g_ref, b_ref, alpha)

    @pl.when(j == pl.num_programs(1) - 1)
    def _():
        st_ref[0] = tail


def _conv_seq(x, hist, w_in, cw, w_out, g, b, *, alpha, tm):
    bsz, t, d = x.shape
    return pl.pallas_call(
        functools.partial(_conv_seq_kernel, alpha=alpha),
        grid=(bsz, t // tm),
        in_specs=[
            pl.BlockSpec((1, tm, d), lambda i, j: (i, j, 0)),
            pl.BlockSpec((1, 2, d), lambda i, j: (i, 0, 0)),
            _const_spec(w_in.shape), _const_spec(cw.shape), _const_spec(w_out.shape),
            _const_spec(g.shape), _const_spec(b.shape),
        ],
        out_specs=[
            pl.BlockSpec((1, tm, d), lambda i, j: (i, j, 0)),
            pl.BlockSpec((1, 2, d), lambda i, j: (i, 0, 0)),
        ],
        out_shape=[jax.ShapeDtypeStruct((bsz, t, d), F32), jax.ShapeDtypeStruct((bsz, 2, d), F32)],
        scratch_shapes=[pltpu.VMEM((2, d), F32)],
        compiler_params=_cparams(2),
        name="conv_seq",
    )(x, hist, w_in, cw, w_out, g, b)


def _conv_tok_kernel(x_ref, hist_ref, w_in_ref, cw_ref, w_out_ref, g_ref, b_ref,
                     o_ref, st_ref, *, alpha):
    x = x_ref[...]
    d = x.shape[-1]
    bch = jnp.dot(x.astype(BF16), w_in_ref[...], preferred_element_type=F32)
    u = bch[:, d:2 * d] * bch[:, 2 * d:]
    h0 = hist_ref[:, :d]
    h1 = hist_ref[:, d:]
    o_ref[...] = _gated_conv_out(x, bch, h1, h0, cw_ref, w_out_ref, g_ref, b_ref, alpha)
    st_ref[:, :d] = h1
    st_ref[:, d:] = u


def _conv_tok(x, hist, w_in, cw, w_out, g, b, *, alpha):
    n, d = x.shape
    args = (x, hist, w_in, cw, w_out, g, b)
    return pl.pallas_call(
        functools.partial(_conv_tok_kernel, alpha=alpha),
        grid=(1,),
        in_specs=[_const_spec(a.shape) for a in args],
        out_specs=[_const_spec((n, d)), _const_spec((n, 2 * d))],
        out_shape=[jax.ShapeDtypeStruct((n, d), F32), jax.ShapeDtypeStruct((n, 2 * d), F32)],
        compiler_params=_cparams(1),
        name="conv_tok",
    )(*args)


def _ffn_kernel(x_ref, wgu_ref, wd_ref, g_ref, b_ref, o_ref, *, alpha):
    x = x_ref[...]
    dff = wd_ref.shape[0]
    gu = jnp.dot(x.astype(BF16), wgu_ref[...], preferred_element_type=F32)
    gate = gu[:, :dff]
    a = (gate * jax.nn.sigmoid(gate)) * gu[:, dff:]
    y = jnp.dot(a.astype(BF16), wd_ref[...], preferred_element_type=F32)
    o_ref[...] = _layer_norm(alpha * x + y, g_ref[...], b_ref[...])


def _ffn(x, wgu, wd, g, b, *, alpha, tm):
    m, d = x.shape
    return pl.pallas_call(
        functools.partial(_ffn_kernel, alpha=alpha),
        grid=(m // tm,),
        in_specs=[
            pl.BlockSpec((tm, d), lambda i: (i, 0)),
            _const_spec(wgu.shape), _const_spec(wd.shape), _const_spec(g.shape), _const_spec(b.shape),
        ],
        out_specs=pl.BlockSpec((tm, d), lambda i: (i, 0)),
        out_shape=jax.ShapeDtypeStruct((m, d), F32),
        compiler_params=_cparams(1),
        name="ffn",
    )(x, wgu, wd, g, b)


def _kv_seq_kernel(x_ref, wkvt_ref, wk_ref, fw_ref, fb_ref, fwt_ref, fbt_ref, place_ref,
                   kt_ref, vt_ref, lft_ref, kb_ref, ak_ref, vtb_ref, ct_ref,
                   carry_ref, carry_t_ref):
    j = pl.program_id(1)

    @pl.when(j == 0)
    def _():
        carry_ref[...] = jnp.zeros_like(carry_ref)
        carry_t_ref[...] = jnp.zeros_like(carry_t_ref)

    xb = x_ref[0].astype(BF16)
    tm, d = xb.shape
    nh = lft_ref.shape[1]
    kvt = lax.dot_general(wkvt_ref[...], xb, _NT, preferred_element_type=F32)
    kt_ref[0] = kvt[:d].reshape(kt_ref.shape[1:])
    vt3 = kvt[d:].reshape(vt_ref.shape[1:])
    vt_ref[0] = vt3
    ones = jnp.ones((nh, V_PAD, tm), F32)
    vtb_ref[0] = jnp.concatenate([vt3, ones], axis=1).reshape(vtb_ref.shape[1:]).astype(BF16)
    kb_ref[0] = jnp.dot(xb, wk_ref[...], preferred_element_type=F32).astype(BF16)

    lft = _log_sigmoid(lax.dot_general(fwt_ref[...], xb, _NT, preferred_element_type=F32) + fbt_ref[...])
    lft_ref[0] = lft
    lane = lax.broadcasted_iota(jnp.int32, lft.shape, 1)
    ct = lft
    shift = 1
    while shift < tm:
        ct = ct + jnp.where(lane >= shift, pltpu.roll(ct, shift, 1), 0.0)
        shift *= 2
    ct = ct + carry_t_ref[...]
    carry_t_ref[...] = ct[:, tm - 1:]
    ct_ref[0] = ct * LOG2E

    lf = _log_sigmoid(jnp.dot(xb, fw_ref[...], preferred_element_type=F32) + fb_ref[...])
    row = lax.broadcasted_iota(jnp.int32, lf.shape, 0)
    c = lf
    shift = 1
    while shift < tm:
        c = c + jnp.where(row >= shift, pltpu.roll(c, shift, 0), 0.0)
        shift *= 2
    c = c + carry_ref[...]
    carry_ref[...] = c[tm - 1:, :]
    hi, mid, lo = _split3(c * LOG2E)
    grp = lax.broadcasted_iota(jnp.int32, lf.shape, 1) // nh
    pieces = jnp.where(grp == 0, hi, jnp.where(grp == 1, mid, jnp.where(grp == 2, lo,
                       jnp.where(grp == N_SPLIT, 1.0, 0.0))))
    ak_ref[0] = jnp.dot(pieces.astype(BF16), place_ref[...], preferred_element_type=F32).astype(BF16)


def _kv_seq(x, wkvt, wk, fw, fb, fwt, fbt, place, *, nh, tm):
    bsz, t, d = x.shape
    dh = d // nh
    vrows = nh * (dh + V_PAD)
    tok = pl.BlockSpec((1, tm, d), lambda i, j: (i, j, 0))
    tr4 = pl.BlockSpec((1, nh, dh, tm), lambda i, j: (i, 0, 0, j))
    tr3 = pl.BlockSpec((1, nh, tm), lambda i, j: (i, 0, j))
    trv = pl.BlockSpec((1, vrows, tm), lambda i, j: (i, 0, j))
    consts = (wkvt, wk, fw, fb, fwt, fbt, place)
    return pl.pallas_call(
        _kv_seq_kernel,
        grid=(bsz, t // tm),
        in_specs=[tok] + [_const_spec(a.shape) for a in consts],
        out_specs=[tr4, tr4, tr3, tok, tok, trv, tr3],
        out_shape=[
            jax.ShapeDtypeStruct((bsz, nh, dh, t), F32), jax.ShapeDtypeStruct((bsz, nh, dh, t), F32),
            jax.ShapeDtypeStruct((bsz, nh, t), F32),
            jax.ShapeDtypeStruct((bsz, t, d), BF16), jax.ShapeDtypeStruct((bsz, t, d), BF16),
            jax.ShapeDtypeStruct((bsz, vrows, t), BF16), jax.ShapeDtypeStruct((bsz, nh, t), F32),
        ],
        scratch_shapes=[pltpu.VMEM((1, fw.shape[1]), F32), pltpu.VMEM((nh, 1), F32)],
        compiler_params=_cparams(2),
        name="kv_seq",
    )(x, *consts)


def _kv_tok_kernel(x_ref, wkv_ref, fw_ref, fb_ref, k_ref, v_ref, lf_ref):
    xb = x_ref[...].astype(BF16)
    d = xb.shape[1]
    nh = lf_ref.shape[-1]
    kv = jnp.dot(xb, wkv_ref[...], preferred_element_type=F32)
    k_ref[...] = kv[:, :d]
    v_ref[...] = kv[:, d:]
    lf = _log_sigmoid(jnp.dot(xb, fw_ref[...], preferred_element_type=F32) + fb_ref[...])
    lf_ref[...] = lf[:, :nh]


def _kv_tok(x, wkv, fw, fb, *, nh):
    n, d = x.shape
    args = (x, wkv, fw, fb)
    return pl.pallas_call(
        _kv_tok_kernel,
        grid=(1,),
        in_specs=[_const_spec(a.shape) for a in args],
        out_specs=[_const_spec((n, d)), _const_spec((n, d)), _const_spec((n, nh))],
        out_shape=[jax.ShapeDtypeStruct((n, d), F32), jax.ShapeDtypeStruct((n, d), F32),
                   jax.ShapeDtypeStruct((n, nh), F32)],
        compiler_params=_cparams(1),
        name="kv_tok",
    )(*args)


def _proj_kernel(x_ref, w_ref, o_ref, *, scale):
    y = jnp.dot(x_ref[...].astype(BF16), w_ref[...], preferred_element_type=F32)
    o_ref[...] = (y * scale).astype(o_ref.dtype)


def _proj(x, w, *, scale, out_dtype, tm):
    m, d = x.shape
    n = w.shape[1]
    return pl.pallas_call(
        functools.partial(_proj_kernel, scale=scale),
        grid=(m // tm,),
        in_specs=[pl.BlockSpec((tm, d), lambda i: (i, 0)), _const_spec(w.shape)],
        out_specs=pl.BlockSpec((tm, n), lambda i: (i, 0)),
        out_shape=jax.ShapeDtypeStruct((m, n), out_dtype),
        compiler_params=_cparams(1),
        name="proj",
    )(x, w)


def _proj_t_kernel(x_ref, wt_ref, o_ref, *, scale):
    y = lax.dot_general(wt_ref[...], x_ref[0].astype(BF16), _NT, preferred_element_type=F32)
    o_ref[0] = (y * scale).astype(o_ref.dtype)


def _proj_t(x, wt, *, scale, tm):
    bsz, t, d = x.shape
    n = wt.shape[0]
    return pl.pallas_call(
        functools.partial(_proj_t_kernel, scale=scale),
        grid=(bsz, t // tm),
        in_specs=[pl.BlockSpec((1, tm, d), lambda i, j: (i, j, 0)), _const_spec(wt.shape)],
        out_specs=pl.BlockSpec((1, n, tm), lambda i, j: (i, 0, j)),
        out_shape=jax.ShapeDtypeStruct((bsz, n, t), BF16),
        compiler_params=_cparams(2),
        name="proj_t",
    )(x, wt)


def _proj_ln_kernel(a_ref, x_ref, w_ref, g_ref, b_ref, o_ref, *, alpha):
    y = jnp.dot(a_ref[...].astype(BF16), w_ref[...], preferred_element_type=F32)
    o_ref[...] = _layer_norm(alpha * x_ref[...] + y, g_ref[...], b_ref[...])


def _proj_ln(a, x, w, g, b, *, alpha, tm):
    m, d = x.shape
    row = lambda i: (i, 0)
    return pl.pallas_call(
        functools.partial(_proj_ln_kernel, alpha=alpha),
        grid=(m // tm,),
        in_specs=[pl.BlockSpec((tm, a.shape[1]), row), pl.BlockSpec((tm, d), row),
                  _const_spec(w.shape), _const_spec(g.shape), _const_spec(b.shape)],
        out_specs=pl.BlockSpec((tm, d), row),
        out_shape=jax.ShapeDtypeStruct((m, d), F32),
        compiler_params=_cparams(1),
        name="proj_ln",
    )(a, x, w, g, b)


def _fa_kernel(qt_ref, k_ref, ak_ref, vt_ref, c_ref, o_ref, qp_ref, m_ref, acc_ref, st_a_ref, st_b_ref,
               *, tq, dh):
    i = pl.program_id(1)
    hpt = LANES // dh
    vrows = dh + V_PAD
    n_tiles = k_ref.shape[2] // LANES
    n_groups = n_tiles // FA_TILES
    sub = lax.broadcasted_iota(jnp.int32, (LANES, tq), 0)
    key_idx = lax.broadcasted_iota(jnp.int32, (tq, tq), 0)
    qry_idx = lax.broadcasted_iota(jnp.int32, (tq, tq), 1)

    def stage(tile, carry):
        lo = pl.multiple_of(tile * LANES, LANES)
        q2 = qt_ref[0, pl.ds(lo, LANES), :].astype(F32)
        c2 = c_ref[0, tile]
        for jh in range(hpt):
            qm = jnp.where((sub >= jh * dh) & (sub < (jh + 1) * dh), q2, 0.0)
            hi, mid, lw = _split3(c2[jh:jh + 1, :])
            base = AUG_PER_HEAD * jh
            aug = jnp.where(sub == base, hi, jnp.where(sub == base + 1, mid, jnp.where(sub == base + 2, lw,
                            jnp.where((sub >= base + N_SPLIT) & (sub < base + AUG_PER_HEAD), 1.0, 0.0))))
            qp_ref[tile * hpt + jh] = jnp.concatenate([qm, aug], axis=0).astype(BF16)
        return carry

    lax.fori_loop(0, n_tiles, stage, 0)
    m_ref[...] = jnp.full_like(m_ref, -jnp.inf)
    acc_ref[...] = jnp.zeros_like(acc_ref)

    def scores(g, kb, st_ref):
        ks = pl.multiple_of(kb * tq, tq)
        for lt in range(FA_TILES):
            tile = g * FA_TILES + lt
            lo = pl.multiple_of(tile * LANES, LANES)
            kk = jnp.concatenate([k_ref[0, pl.ds(ks, tq), pl.ds(lo, LANES)],
                                  ak_ref[0, pl.ds(ks, tq), pl.ds(lo, LANES)]], axis=1)
            for jh in range(hpt):
                st_ref[lt * hpt + jh] = jnp.dot(kk, qp_ref[tile * hpt + jh], preferred_element_type=F32)

    def accumulate(g, kb, st_ref, masked):
        ks = pl.multiple_of(kb * tq, tq)
        for lt in range(FA_TILES):
            tile = g * FA_TILES + lt
            v2 = vt_ref[0, pl.ds(pl.multiple_of(tile * (hpt * vrows), hpt * vrows), hpt * vrows), pl.ds(ks, tq)]
            for jh in range(hpt):
                head = tile * hpt + jh
                st = st_ref[lt * hpt + jh]
                if masked:
                    st = jnp.where(key_idx <= qry_idx, st, -jnp.inf)
                m_old = m_ref[head]
                m_new = jnp.maximum(m_old, jnp.max(st, axis=0, keepdims=True))
                p = jnp.exp2(st - m_new).astype(BF16)
                pv = jnp.dot(v2[jh * vrows:(jh + 1) * vrows, :], p, preferred_element_type=F32)
                acc_ref[head] = jnp.exp2(m_old - m_new) * acc_ref[head] + pv
                m_ref[head] = m_new

    def run_group(g, st_even_ref, st_odd_ref, i_is_odd):
        def two_steps(tt, c):
            kb = 2 * tt
            scores(g, kb + 1, st_odd_ref)
            accumulate(g, kb, st_even_ref, False)
            scores(g, kb + 2, st_even_ref)
            accumulate(g, kb + 1, st_odd_ref, False)
            return c

        lax.fori_loop(0, i // 2, two_steps, 0)
        g_next = jnp.minimum(g + 1, n_groups - 1)
        if i_is_odd:
            scores(g, i, st_odd_ref)
            accumulate(g, i - 1, st_even_ref, False)
            scores(g_next, 0, st_even_ref)
            accumulate(g, i, st_odd_ref, True)
        else:
            scores(g_next, 0, st_odd_ref)
            accumulate(g, i, st_even_ref, True)

    scores(0, 0, st_a_ref)

    @pl.when(i % 2 == 1)
    def _():
        def one_group(g, c):
            run_group(g, st_a_ref, st_b_ref, True)
            return c
        lax.fori_loop(0, n_groups, one_group, 0)

    @pl.when(i % 2 == 0)
    def _():
        def two_groups(gp, c):
            run_group(2 * gp, st_a_ref, st_b_ref, False)
            run_group(2 * gp + 1, st_b_ref, st_a_ref, False)
            return c
        lax.fori_loop(0, n_groups // 2, two_groups, 0)

    def finish(tile, carry):
        lo = pl.multiple_of(tile * LANES, LANES)
        heads = [acc_ref[tile * hpt + jh] for jh in range(hpt)]
        ot = jnp.concatenate([acc[:dh] / acc[dh:dh + 1] for acc in heads], axis=0)
        o_ref[0, :, pl.ds(lo, LANES)] = ot.T.astype(o_ref.dtype)
        return carry

    lax.fori_loop(0, n_tiles, finish, 0)


def _flash_attn(qt, kb, ak, vtb, ct, *, nh, tq):
    bsz, d, t = qt.shape
    dh = d // nh
    hpt = LANES // dh
    n_tiles = nh // hpt
    heads_per_group = FA_TILES * hpt
    assert n_tiles % (2 * FA_TILES) == 0
    res3 = lambda shape: pl.BlockSpec(shape, lambda b, i: (b, 0, 0), pipeline_mode=pl.Buffered(1))
    return pl.pallas_call(
        functools.partial(_fa_kernel, tq=tq, dh=dh),
        grid=(bsz, t // tq),
        in_specs=[
            pl.BlockSpec((1, d, tq), lambda b, i: (b, 0, i)),
            res3((1, t, d)), res3((1, t, d)), res3((1, vtb.shape[1], t)),
            pl.BlockSpec((1, n_tiles, hpt, tq), lambda b, i: (b, 0, 0, i)),
        ],
        out_specs=pl.BlockSpec((1, tq, d), lambda b, i: (b, i, 0)),
        out_shape=jax.ShapeDtypeStruct((bsz, t, d), BF16),
        scratch_shapes=[pltpu.VMEM((nh, 2 * LANES, tq), BF16),
                        pltpu.VMEM((nh, 1, tq), F32), pltpu.VMEM((nh, dh + V_PAD, tq), F32),
                        pltpu.VMEM((heads_per_group, tq, tq), F32), pltpu.VMEM((heads_per_group, tq, tq), F32)],
        compiler_params=_cparams(2),
        name="flash_attn",
    )(qt, kb, ak, vtb, ct.reshape(bsz, n_tiles, hpt, t))


def _decode_kernel(pt_ref, q_ref, kn_ref, vn_ref, lfn_ref, *refs, pages, nh, dh):
    del pt_ref
    kt_refs = refs[:pages]
    vt_refs = refs[pages:2 * pages]
    lf_refs = refs[2 * pages:3 * pages]
    o_ref = refs[3 * pages]
    qbd_ref, m_ref, l_ref, acc_ref, r_ref = refs[3 * pages + 1:]
    c = pl.program_id(1)
    d = nh * dh
    page = kt_refs[0].shape[2]
    head_row = lax.broadcasted_iota(jnp.int32, (nh, d), 0)
    head_of_lane = lax.broadcasted_iota(jnp.int32, (nh, d), 1) // dh
    block_diag = head_of_lane == head_row

    @pl.when(c == 0)
    def _():
        qbd = jnp.where(block_diag, jnp.broadcast_to(q_ref[0], (nh, d)), 0.0)
        qbd_ref[...] = qbd.astype(BF16)
        m_ref[...] = jnp.sum(qbd * kn_ref[0], axis=-1, keepdims=True)
        l_ref[...] = jnp.ones_like(l_ref)
        acc_ref[...] = jnp.broadcast_to(vn_ref[0], (nh, d))
        eye = (lax.broadcasted_iota(jnp.int32, (nh, nh), 0)
               == lax.broadcasted_iota(jnp.int32, (nh, nh), 1))
        lfn = jnp.broadcast_to(lfn_ref[0], (nh, nh))
        r_ref[...] = jnp.sum(jnp.where(eye, lfn, 0.0), axis=-1, keepdims=True)

    qbd = qbd_ref[...]
    r = r_ref[...]
    key_lane = lax.broadcasted_iota(jnp.int32, (nh, page), 1)
    scores = [None] * pages
    for pg in reversed(range(pages)):
        s = jnp.dot(qbd, kt_refs[pg][0].astype(BF16), preferred_element_type=F32)
        lft = lf_refs[pg][0]
        inc = lft
        shift = 1
        while shift < page:
            moved = pltpu.roll(inc, page - shift, 1)
            inc = inc + jnp.where(key_lane < page - shift, moved, 0.0)
            shift *= 2
        scores[pg] = s + ((inc - lft) + r)
        r = r + inc[:, 0:1]
    r_ref[...] = r
    s_all = jnp.concatenate(scores, axis=1)
    m_old = m_ref[...]
    m_new = jnp.maximum(m_old, jnp.max(s_all, axis=-1, keepdims=True))
    a = jnp.exp(m_old - m_new)
    p = jnp.exp(s_all - m_new)
    l_ref[...] = a * l_ref[...] + jnp.sum(p, axis=-1, keepdims=True)
    p = p.astype(BF16)
    pv = None
    for pg in range(pages):
        part = lax.dot_general(p[:, pg * page:(pg + 1) * page], vt_refs[pg][0].astype(BF16), _NT,
                               preferred_element_type=F32)
        pv = part if pv is None else pv + part
    acc_ref[...] = a * acc_ref[...] + pv
    m_ref[...] = m_new

    @pl.when(c == pl.num_programs(1) - 1)
    def _():
        o_ref[0] = jnp.sum(jnp.where(block_diag, acc_ref[...] / l_ref[...], 0.0), axis=0, keepdims=True)


def _decode_attn(page_table, q, k_new, v_new, lf_new, cache_kt, cache_vt, cache_lft, *, nh, pages):
    n, d = q.shape
    dh = d // nh
    page = cache_lft.shape[2]
    n_chunks = page_table.shape[1] // pages

    def tok_spec(width):
        return pl.BlockSpec((1, 1, width), lambda b, c, pt: (b, 0, 0))

    def page_spec(rows, pg):
        return pl.BlockSpec(
            (1, rows, page),
            lambda b, c, pt: (pt[b, (n_chunks - 1 - c) * pages + pg], 0, 0))

    in_specs = [tok_spec(d), tok_spec(d), tok_spec(d), tok_spec(nh)]
    in_specs += [page_spec(d, pg) for pg in range(pages)]
    in_specs += [page_spec(d, pg) for pg in range(pages)]
    in_specs += [page_spec(nh, pg) for pg in range(pages)]
    grid_spec = pltpu.PrefetchScalarGridSpec(
        num_scalar_prefetch=1,
        grid=(n, n_chunks),
        in_specs=in_specs,
        out_specs=tok_spec(d),
        scratch_shapes=[
            pltpu.VMEM((nh, d), BF16), pltpu.VMEM((nh, 1), F32), pltpu.VMEM((nh, 1), F32),
            pltpu.VMEM((nh, d), F32), pltpu.VMEM((nh, 1), F32),
        ],
    )
    out = pl.pallas_call(
        functools.partial(_decode_kernel, pages=pages, nh=nh, dh=dh),
        grid_spec=grid_spec,
        out_shape=jax.ShapeDtypeStruct((n, 1, d), F32),
        compiler_params=_cparams(2),
        name="decode_attn",
    )(page_table, q.reshape(n, 1, d), k_new.reshape(n, 1, d), v_new.reshape(n, 1, d),
      lf_new.reshape(n, 1, nh), *([cache_kt] * pages), *([cache_vt] * pages), *([cache_lft] * pages))
    return out.reshape(n, d)


def _bias_placement(nh, dh, d):
    hpt = LANES // dh
    place = np.zeros((LANES, d), np.float32)
    for h in range(nh):
        col0 = (h // hpt) * LANES + AUG_PER_HEAD * (h % hpt)
        for piece in range(N_SPLIT):
            place[N_SPLIT * nh, col0 + piece] = 1.0
            place[piece * nh + h, col0 + N_SPLIT + piece] = -1.0
    return place


def kernel(x_prompt, x_sample, cache_k, cache_v, cache_logf, state_conv, page_table, a_w_in, a_conv_w, a_w_out, b_w_q, b_w_o, kv_w, f_w, f_b, ffn_w_gu, ffn_w_down, ln_g, ln_b):
    bsz, t, d = x_prompt.shape
    n_s = x_sample.shape[0]
    n_a, n_b = a_w_in.shape[0], b_w_q.shape[0]
    depth = n_a + n_b
    nh = f_w.shape[1]
    dh = d // nh
    n_pool, page = cache_logf.shape[0], cache_logf.shape[1]
    alpha = (2 * depth) ** 0.25
    attn_scale = dh ** -0.5
    m = bsz * t
    tm_seq, tm_ffn, tq, dec_pages = 512, 512, 512, 16
    assert LANES % dh == 0 and (N_SPLIT + 1) * nh <= LANES and AUG_PER_HEAD * (LANES // dh) <= LANES

    w_in, w_out = a_w_in.astype(BF16), a_w_out.astype(BF16)
    w_q, w_o, w_kv = b_w_q.astype(BF16), b_w_o.astype(BF16), kv_w.astype(BF16)
    w_qt = jnp.swapaxes(b_w_q, 1, 2).astype(BF16)
    w_kvt = kv_w.T.astype(BF16)
    w_gu, w_down = ffn_w_gu.astype(BF16), ffn_w_down.astype(BF16)
    fw_pad = jnp.pad(f_w, ((0, 0), (0, LANES - nh))).astype(BF16)
    fb_pad = jnp.pad(f_b, (0, LANES - nh)).reshape(1, LANES)
    fw_rep = jnp.pad(jnp.tile(f_w, (1, N_SPLIT)), ((0, 0), (0, LANES - N_SPLIT * nh))).astype(BF16)
    fb_rep = jnp.pad(jnp.tile(f_b, N_SPLIT), (0, LANES - N_SPLIT * nh)).reshape(1, LANES)
    fwt = f_w.T.astype(BF16)
    fbt = f_b.reshape(nh, 1)
    place = jnp.asarray(_bias_placement(nh, dh, d), BF16)
    g = ln_g.reshape(depth, 2, 1, d)
    b = ln_b.reshape(depth, 2, 1, d)

    x = x_prompt
    hist0 = jnp.zeros((bsz, 2, d), F32)
    conv_p = []
    for l in range(n_a):
        x, st = _conv_seq(x, hist0, w_in[l], a_conv_w[l], w_out[l], g[l, 0], b[l, 0], alpha=alpha, tm=tm_seq)
        conv_p.append(st)
        x = _ffn(x.reshape(m, d), w_gu[l], w_down[l], g[l, 1], b[l, 1], alpha=alpha, tm=tm_ffn).reshape(bsz, t, d)
    kt_p, vt_p, lft_p, kb, ak, vtb, ct = _kv_seq(x, w_kvt, w_kv[:, :d], fw_rep, fb_rep, fwt, fbt, place,
                                                 nh=nh, tm=tm_seq)
    x = x.reshape(m, d)
    for j in range(n_b):
        l = n_a + j
        qt = _proj_t(x.reshape(bsz, t, d), w_qt[j], scale=attn_scale * LOG2E, tm=tm_seq)
        o = _flash_attn(qt, kb, ak, vtb, ct, nh=nh, tq=tq)
        x = _proj_ln(o.reshape(m, d), x, w_o[j], g[l, 0], b[l, 0], alpha=alpha, tm=tm_seq)
        x = _ffn(x, w_gu[l], w_down[l], g[l, 1], b[l, 1], alpha=alpha, tm=tm_ffn)
    y_prompt = x.reshape(bsz, t, d)

    cache_kt = jnp.transpose(cache_k, (0, 2, 3, 1)).reshape(n_pool, d, page)
    cache_vt = jnp.transpose(cache_v, (0, 2, 3, 1)).reshape(n_pool, d, page)
    cache_lft = jnp.transpose(cache_logf, (0, 2, 1))
    xs = x_sample.reshape(n_s, d)
    conv_s = []
    for l in range(n_a):
        xs, st = _conv_tok(xs, state_conv[l].reshape(n_s, 2 * d), w_in[l], a_conv_w[l], w_out[l],
                           g[l, 0], b[l, 0], alpha=alpha)
        conv_s.append(st.reshape(n_s, 2, d))
        xs = _ffn(xs, w_gu[l], w_down[l], g[l, 1], b[l, 1], alpha=alpha, tm=n_s)
    k_s, v_s, lf_s = _kv_tok(xs, w_kv, fw_pad, fb_pad, nh=nh)
    for j in range(n_b):
        l = n_a + j
        qs = _proj(xs, w_q[j], scale=attn_scale, out_dtype=F32, tm=n_s)
        os_ = _decode_attn(page_table, qs, k_s, v_s, lf_s, cache_kt, cache_vt, cache_lft, nh=nh, pages=dec_pages)
        xs = _proj_ln(os_, xs, w_o[j], g[l, 0], b[l, 0], alpha=alpha, tm=n_s)
        xs = _ffn(xs, w_gu[l], w_down[l], g[l, 1], b[l, 1], alpha=alpha, tm=n_s)

    return (y_prompt, xs.reshape(n_s, 1, d),
            jnp.transpose(kt_p, (0, 3, 1, 2)), jnp.transpose(vt_p, (0, 3, 1, 2)), jnp.transpose(lft_p, (0, 2, 1)),
            jnp.stack(conv_p),
            k_s.reshape(n_s, 1, nh, dh), v_s.reshape(n_s, 1, nh, dh), lf_s.reshape(n_s, 1, nh),
            jnp.stack(conv_s))
```

```python
import functools
import math

import numpy as np
import jax
import jax.numpy as jnp
from jax import lax
from jax.experimental import pallas as pl
from jax.experimental.pallas import tpu as pltpu

F32 = jnp.float32
BF16 = jnp.bfloat16
LN_EPS = 1e-5
LOG2E = math.log2(math.e)
V7X_VMEM_LIMIT_BYTES = 56 * 1024 * 1024
LANES = 128
N_SPLIT = 3
AUG_PER_HEAD = 2 * N_SPLIT
V_PAD = 16
FA_TILES = 2

_NT = (((1,), (1,)), ((), ()))


def _cparams(n_axes):
    return pltpu.CompilerParams(
        dimension_semantics=("arbitrary",) * n_axes,
        vmem_limit_bytes=V7X_VMEM_LIMIT_BYTES,
    )


def _const_spec(shape):
    nd = len(shape)
    return pl.BlockSpec(shape, lambda *_: (0,) * nd, pipeline_mode=pl.Buffered(1))


def _layer_norm(r, g, b):
    mu = jnp.mean(r, axis=-1, keepdims=True)
    c = r - mu
    var = jnp.mean(c * c, axis=-1, keepdims=True)
    return c * lax.rsqrt(var + LN_EPS) * g + b


def _log_sigmoid(z):
    return jnp.minimum(z, 0.0) - jnp.log1p(jnp.exp(-jnp.abs(z)))


def _split3(c):
    hi = c.astype(BF16).astype(F32)
    r = c - hi
    mid = r.astype(BF16).astype(F32)
    lo = (r - mid).astype(BF16).astype(F32)
    return hi, mid, lo


def _gated_conv_out(x, bch, p1, p2, cw_ref, w_out_ref, g_ref, b_ref, alpha):
    d = x.shape[-1]
    u = bch[:, d:2 * d] * bch[:, 2 * d:]
    z = cw_ref[2:3, :] * u
    z = z + cw_ref[0:1, :] * p2
    z = z + cw_ref[1:2, :] * p1
    y = jnp.dot((bch[:, :d] * z).astype(BF16), w_out_ref[...], preferred_element_type=F32)
    return _layer_norm(alpha * x + y, g_ref[...], b_ref[...])


def _ffn_value(x, wgu_ref, wd_ref, g_ref, b_ref, alpha):
    dff = wd_ref.shape[0]
    gu = jnp.dot(x.astype(BF16), wgu_ref[...], preferred_element_type=F32)
    gate = gu[:, :dff]
    a = (gate * jax.nn.sigmoid(gate)) * gu[:, dff:]
    y = jnp.dot(a.astype(BF16), wd_ref[...], preferred_element_type=F32)
    return _layer_norm(alpha * x + y, g_ref[...], b_ref[...])


def _q_transposed(x, wqt_ref, scale):
    y = lax.dot_general(wqt_ref[...], x.astype(BF16), _NT, preferred_element_type=F32)
    return (y * scale).astype(BF16)


def _conv_seq_kernel(x_ref, hist_ref, w_in_ref, cw_ref, w_out_ref, g_ref, b_ref, wgu_ref, wd_ref, g1_ref, b1_ref,
                     o_ref, st_ref, carry_ref, *, alpha):
    j = pl.program_id(1)

    @pl.when(j == 0)
    def _():
        carry_ref[...] = hist_ref[0]

    x = x_ref[0]
    tm, d = x.shape
    bch = jnp.dot(x.astype(BF16), w_in_ref[...], preferred_element_type=F32)
    u = bch[:, d:2 * d] * bch[:, 2 * d:]
    h0 = carry_ref[0:1, :]
    h1 = carry_ref[1:2, :]
    row = lax.broadcasted_iota(jnp.int32, (tm, d), 0)
    p1 = jnp.where(row == 0, h1, pltpu.roll(u, 1, 0))
    p2 = jnp.where(row == 0, h0, jnp.where(row == 1, h1, pltpu.roll(u, 2, 0)))
    tail = u[tm - 2:, :]
    carry_ref[...] = tail
    x1 = _gated_conv_out(x, bch, p1, p2, cw_ref, w_out_ref, g_ref, b_ref, alpha)
    o_ref[0] = _ffn_value(x1, wgu_ref, wd_ref, g1_ref, b1_ref, alpha)

    @pl.when(j == pl.num_programs(1) - 1)
    def _():
        st_ref[0] = tail


def _conv_seq(x, hist, w_in, cw, w_out, g, b, wgu, wd, g1, b1, *, alpha, tm):
    bsz, t, d = x.shape
    consts = (w_in, cw, w_out, g, b, wgu, wd, g1, b1)
    return pl.pallas_call(
        functools.partial(_conv_seq_kernel, alpha=alpha),
        grid=(bsz, t // tm),
        in_specs=[
            pl.BlockSpec((1, tm, d), lambda i, j: (i, j, 0)),
            pl.BlockSpec((1, 2, d), lambda i, j: (i, 0, 0)),
        ] + [_const_spec(c.shape) for c in consts],
        out_specs=[
            pl.BlockSpec((1, tm, d), lambda i, j: (i, j, 0)),
            pl.BlockSpec((1, 2, d), lambda i, j: (i, 0, 0)),
        ],
        out_shape=[jax.ShapeDtypeStruct((bsz, t, d), F32), jax.ShapeDtypeStruct((bsz, 2, d), F32)],
        scratch_shapes=[pltpu.VMEM((2, d), F32)],
        compiler_params=_cparams(2),
        name="conv_seq",
    )(x, hist, *consts)


def _conv_tok_kernel(x_ref, hist_ref, w_in_ref, cw_ref, w_out_ref, g_ref, b_ref,
                     o_ref, st_ref, *, alpha):
    x = x_ref[...]
    d = x.shape[-1]
    bch = jnp.dot(x.astype(BF16), w_in_ref[...], preferred_element_type=F32)
    u = bch[:, d:2 * d] * bch[:, 2 * d:]
    h0 = hist_ref[:, :d]
    h1 = hist_ref[:, d:]
    o_ref[...] = _gated_conv_out(x, bch, h1, h0, cw_ref, w_out_ref, g_ref, b_ref, alpha)
    st_ref[:, :d] = h1
    st_ref[:, d:] = u


def _conv_tok(x, hist, w_in, cw, w_out, g, b, *, alpha):
    n, d = x.shape
    args = (x, hist, w_in, cw, w_out, g, b)
    return pl.pallas_call(
        functools.partial(_conv_tok_kernel, alpha=alpha),
        grid=(1,),
        in_specs=[_const_spec(a.shape) for a in args],
        out_specs=[_const_spec((n, d)), _const_spec((n, 2 * d))],
        out_shape=[jax.ShapeDtypeStruct((n, d), F32), jax.ShapeDtypeStruct((n, 2 * d), F32)],
        compiler_params=_cparams(1),
        name="conv_tok",
    )(*args)


def _ffn_kernel(x_ref, wgu_ref, wd_ref, g_ref, b_ref, o_ref, *, alpha):
    o_ref[...] = _ffn_value(x_ref[...], wgu_ref, wd_ref, g_ref, b_ref, alpha)


def _ffn(x, wgu, wd, g, b, *, alpha, tm):
    m, d = x.shape
    return pl.pallas_call(
        functools.partial(_ffn_kernel, alpha=alpha),
        grid=(m // tm,),
        in_specs=[
            pl.BlockSpec((tm, d), lambda i: (i, 0)),
            _const_spec(wgu.shape), _const_spec(wd.shape), _const_spec(g.shape), _const_spec(b.shape),
        ],
        out_specs=pl.BlockSpec((tm, d), lambda i: (i, 0)),
        out_shape=jax.ShapeDtypeStruct((m, d), F32),
        compiler_params=_cparams(1),
        name="ffn",
    )(x, wgu, wd, g, b)


def _kv_seq_kernel(x_ref, wkvt_ref, wk_ref, fw_ref, fb_ref, fwt_ref, fbt_ref, place_ref, wqt_ref,
                   kt_ref, vt_ref, lft_ref, kb_ref, ak_ref, vtb_ref, ct_ref, qt_ref,
                   carry_ref, carry_t_ref, *, q_scale):
    j = pl.program_id(1)

    @pl.when(j == 0)
    def _():
        carry_ref[...] = jnp.zeros_like(carry_ref)
        carry_t_ref[...] = jnp.zeros_like(carry_t_ref)

    xb = x_ref[0].astype(BF16)
    tm, d = xb.shape
    nh = lft_ref.shape[1]
    kvt = lax.dot_general(wkvt_ref[...], xb, _NT, preferred_element_type=F32)
    kt_ref[0] = kvt[:d].reshape(kt_ref.shape[1:])
    vt3 = kvt[d:].reshape(vt_ref.shape[1:])
    vt_ref[0] = vt3
    ones = jnp.ones((nh, V_PAD, tm), F32)
    vtb_ref[0] = jnp.concatenate([vt3, ones], axis=1).reshape(vtb_ref.shape[1:]).astype(BF16)
    kb_ref[0] = jnp.dot(xb, wk_ref[...], preferred_element_type=F32).astype(BF16)
    qt_ref[0] = _q_transposed(xb, wqt_ref, q_scale)

    lft = _log_sigmoid(lax.dot_general(fwt_ref[...], xb, _NT, preferred_element_type=F32) + fbt_ref[...])
    lft_ref[0] = lft
    lane = lax.broadcasted_iota(jnp.int32, lft.shape, 1)
    ct = lft
    shift = 1
    while shift < tm:
        ct = ct + jnp.where(lane >= shift, pltpu.roll(ct, shift, 1), 0.0)
        shift *= 2
    ct = ct + carry_t_ref[...]
    carry_t_ref[...] = ct[:, tm - 1:]
    ct_ref[0] = ct * LOG2E

    lf = _log_sigmoid(jnp.dot(xb, fw_ref[...], preferred_element_type=F32) + fb_ref[...])
    row = lax.broadcasted_iota(jnp.int32, lf.shape, 0)
    c = lf
    shift = 1
    while shift < tm:
        c = c + jnp.where(row >= shift, pltpu.roll(c, shift, 0), 0.0)
        shift *= 2
    c = c + carry_ref[...]
    carry_ref[...] = c[tm - 1:, :]
    hi, mid, lo = _split3(c * LOG2E)
    grp = lax.broadcasted_iota(jnp.int32, lf.shape, 1) // nh
    pieces = jnp.where(grp == 0, hi, jnp.where(grp == 1, mid, jnp.where(grp == 2, lo,
                       jnp.where(grp == N_SPLIT, 1.0, 0.0))))
    ak_ref[0] = jnp.dot(pieces.astype(BF16), place_ref[...], preferred_element_type=F32).astype(BF16)


def _kv_seq(x, wkvt, wk, fw, fb, fwt, fbt, place, wqt, *, nh, tm, q_scale):
    bsz, t, d = x.shape
    dh = d // nh
    vrows = nh * (dh + V_PAD)
    tok = pl.BlockSpec((1, tm, d), lambda i, j: (i, j, 0))
    tr4 = pl.BlockSpec((1, nh, dh, tm), lambda i, j: (i, 0, 0, j))
    tr3 = pl.BlockSpec((1, nh, tm), lambda i, j: (i, 0, j))
    trv = pl.BlockSpec((1, vrows, tm), lambda i, j: (i, 0, j))
    trq = pl.BlockSpec((1, wqt.shape[0], tm), lambda i, j: (i, 0, j))
    consts = (wkvt, wk, fw, fb, fwt, fbt, place, wqt)
    return pl.pallas_call(
        functools.partial(_kv_seq_kernel, q_scale=q_scale),
        grid=(bsz, t // tm),
        in_specs=[tok] + [_const_spec(a.shape) for a in consts],
        out_specs=[tr4, tr4, tr3, tok, tok, trv, tr3, trq],
        out_shape=[
            jax.ShapeDtypeStruct((bsz, nh, dh, t), F32), jax.ShapeDtypeStruct((bsz, nh, dh, t), F32),
            jax.ShapeDtypeStruct((bsz, nh, t), F32),
            jax.ShapeDtypeStruct((bsz, t, d), BF16), jax.ShapeDtypeStruct((bsz, t, d), BF16),
            jax.ShapeDtypeStruct((bsz, vrows, t), BF16), jax.ShapeDtypeStruct((bsz, nh, t), F32),
            jax.ShapeDtypeStruct((bsz, wqt.shape[0], t), BF16),
        ],
        scratch_shapes=[pltpu.VMEM((1, fw.shape[1]), F32), pltpu.VMEM((nh, 1), F32)],
        compiler_params=_cparams(2),
        name="kv_seq",
    )(x, *consts)


def _kv_tok_kernel(x_ref, wkv_ref, fw_ref, fb_ref, k_ref, v_ref, lf_ref):
    xb = x_ref[...].astype(BF16)
    d = xb.shape[1]
    nh = lf_ref.shape[-1]
    kv = jnp.dot(xb, wkv_ref[...], preferred_element_type=F32)
    k_ref[...] = kv[:, :d]
    v_ref[...] = kv[:, d:]
    lf = _log_sigmoid(jnp.dot(xb, fw_ref[...], preferred_element_type=F32) + fb_ref[...])
    lf_ref[...] = lf[:, :nh]


def _kv_tok(x, wkv, fw, fb, *, nh):
    n, d = x.shape
    args = (x, wkv, fw, fb)
    return pl.pallas_call(
        _kv_tok_kernel,
        grid=(1,),
        in_specs=[_const_spec(a.shape) for a in args],
        out_specs=[_const_spec((n, d)), _const_spec((n, d)), _const_spec((n, nh))],
        out_shape=[jax.ShapeDtypeStruct((n, d), F32), jax.ShapeDtypeStruct((n, d), F32),
                   jax.ShapeDtypeStruct((n, nh), F32)],
        compiler_params=_cparams(1),
        name="kv_tok",
    )(*args)


def _proj_kernel(x_ref, w_ref, o_ref, *, scale):
    y = jnp.dot(x_ref[...].astype(BF16), w_ref[...], preferred_element_type=F32)
    o_ref[...] = (y * scale).astype(o_ref.dtype)


def _proj(x, w, *, scale, out_dtype, tm):
    m, d = x.shape
    n = w.shape[1]
    return pl.pallas_call(
        functools.partial(_proj_kernel, scale=scale),
        grid=(m // tm,),
        in_specs=[pl.BlockSpec((tm, d), lambda i: (i, 0)), _const_spec(w.shape)],
        out_specs=pl.BlockSpec((tm, n), lambda i: (i, 0)),
        out_shape=jax.ShapeDtypeStruct((m, n), out_dtype),
        compiler_params=_cparams(1),
        name="proj",
    )(x, w)


def _proj_ln_kernel(a_ref, x_ref, w_ref, g_ref, b_ref, o_ref, *, alpha):
    y = jnp.dot(a_ref[...].astype(BF16), w_ref[...], preferred_element_type=F32)
    o_ref[...] = _layer_norm(alpha * x_ref[...] + y, g_ref[...], b_ref[...])


def _proj_ln(a, x, w, g, b, *, alpha, tm):
    m, d = x.shape
    row = lambda i: (i, 0)
    return pl.pallas_call(
        functools.partial(_proj_ln_kernel, alpha=alpha),
        grid=(m // tm,),
        in_specs=[pl.BlockSpec((tm, a.shape[1]), row), pl.BlockSpec((tm, d), row),
                  _const_spec(w.shape), _const_spec(g.shape), _const_spec(b.shape)],
        out_specs=pl.BlockSpec((tm, d), row),
        out_shape=jax.ShapeDtypeStruct((m, d), F32),
        compiler_params=_cparams(1),
        name="proj_ln",
    )(a, x, w, g, b)


def _attn_ffn_kernel(a_ref, x_ref, w_ref, g_ref, b_ref, wgu_ref, wd_ref, g1_ref, b1_ref, *refs, alpha, q_scale):
    y = jnp.dot(a_ref[0], w_ref[...], preferred_element_type=F32)
    x1 = _layer_norm(alpha * x_ref[0] + y, g_ref[...], b_ref[...])
    x2 = _ffn_value(x1, wgu_ref, wd_ref, g1_ref, b1_ref, alpha)
    if len(refs) == 1:
        (o_ref,) = refs
    else:
        wqt_ref, o_ref, qt_ref = refs
        qt_ref[0] = _q_transposed(x2, wqt_ref, q_scale)
    o_ref[0] = x2


def _attn_ffn(a, x, w, g, b, wgu, wd, g1, b1, wqt, *, alpha, tm, q_scale):
    bsz, t, d = x.shape
    tok = pl.BlockSpec((1, tm, d), lambda i, j: (i, j, 0))
    consts = (w, g, b, wgu, wd, g1, b1) + (() if wqt is None else (wqt,))
    out_specs, out_shape = [tok], [jax.ShapeDtypeStruct((bsz, t, d), F32)]
    if wqt is not None:
        out_specs.append(pl.BlockSpec((1, wqt.shape[0], tm), lambda i, j: (i, 0, j)))
        out_shape.append(jax.ShapeDtypeStruct((bsz, wqt.shape[0], t), BF16))
    out = pl.pallas_call(
        functools.partial(_attn_ffn_kernel, alpha=alpha, q_scale=q_scale),
        grid=(bsz, t // tm),
        in_specs=[tok, tok] + [_const_spec(c.shape) for c in consts],
        out_specs=out_specs,
        out_shape=out_shape,
        compiler_params=_cparams(2),
        name="attn_ffn",
    )(a, x, *consts)
    return out[0] if wqt is None else out


def _fa_kernel(qt_ref, k_ref, ak_ref, vt_ref, c_ref, o_ref, m_ref, acc_ref, st_a_ref, st_b_ref, *, tq, dh):
    i = pl.program_id(1)
    hpt = LANES // dh
    vrows = dh + V_PAD
    n_groups = k_ref.shape[2] // (LANES * FA_TILES)
    sub = lax.broadcasted_iota(jnp.int32, (LANES, tq), 0)
    key_idx = lax.broadcasted_iota(jnp.int32, (tq, tq), 0)
    qry_idx = lax.broadcasted_iota(jnp.int32, (tq, tq), 1)

    def group(gi, carry):
        tiles = [gi * FA_TILES + lt for lt in range(FA_TILES)]
        los = [pl.multiple_of(tile * LANES, LANES) for tile in tiles]
        qp = []
        for tile, lo in zip(tiles, los):
            q2 = qt_ref[0, pl.ds(lo, LANES), :].astype(F32)
            c2 = c_ref[0, tile]
            for jh in range(hpt):
                qm = jnp.where((sub >= jh * dh) & (sub < (jh + 1) * dh), q2, 0.0)
                hi, mid, lw = _split3(c2[jh:jh + 1, :])
                base = AUG_PER_HEAD * jh
                aug = jnp.where(sub == base, hi, jnp.where(sub == base + 1, mid, jnp.where(sub == base + 2, lw,
                                jnp.where((sub >= base + N_SPLIT) & (sub < base + AUG_PER_HEAD), 1.0, 0.0))))
                qp.append(jnp.concatenate([qm, aug], axis=0).astype(BF16))
        m_ref[...] = jnp.full_like(m_ref, -jnp.inf)
        acc_ref[...] = jnp.zeros_like(acc_ref)

        def scores(kb, st_ref):
            ks = pl.multiple_of(kb * tq, tq)
            for lt, lo in enumerate(los):
                kk = jnp.concatenate([k_ref[0, pl.ds(ks, tq), pl.ds(lo, LANES)],
                                      ak_ref[0, pl.ds(ks, tq), pl.ds(lo, LANES)]], axis=1)
                for jh in range(hpt):
                    st_ref[lt * hpt + jh] = jnp.dot(kk, qp[lt * hpt + jh], preferred_element_type=F32)

        def accumulate(kb, st_ref, masked):
            ks = pl.multiple_of(kb * tq, tq)
            for lt, tile in enumerate(tiles):
                v2 = vt_ref[0, pl.ds(pl.multiple_of(tile * (hpt * vrows), hpt * vrows), hpt * vrows), pl.ds(ks, tq)]
                for jh in range(hpt):
                    hh = lt * hpt + jh
                    st = st_ref[hh]
                    if masked:
                        st = jnp.where(key_idx <= qry_idx, st, -jnp.inf)
                    m_old = m_ref[hh]
                    m_new = jnp.maximum(m_old, jnp.max(st, axis=0, keepdims=True))
                    p = jnp.exp2(st - m_new).astype(BF16)
                    pv = jnp.dot(v2[jh * vrows:(jh + 1) * vrows, :], p, preferred_element_type=F32)
                    acc_ref[hh] = jnp.exp2(m_old - m_new) * acc_ref[hh] + pv
                    m_ref[hh] = m_new

        def two_steps(tt, c):
            kb = 2 * tt
            scores(kb + 1, st_b_ref)
            accumulate(kb, st_a_ref, False)
            scores(kb + 2, st_a_ref)
            accumulate(kb + 1, st_b_ref, False)
            return c

        scores(0, st_a_ref)
        lax.fori_loop(0, i // 2, two_steps, 0)

        @pl.when(i % 2 == 1)
        def _():
            scores(i, st_b_ref)
            accumulate(i - 1, st_a_ref, False)
            accumulate(i, st_b_ref, True)

        @pl.when(i % 2 == 0)
        def _():
            accumulate(i, st_a_ref, True)

        for lt, lo in enumerate(los):
            heads = [acc_ref[lt * hpt + jh] for jh in range(hpt)]
            ot = jnp.concatenate([acc[:dh] / acc[dh:dh + 1] for acc in heads], axis=0)
            o_ref[0, :, pl.ds(lo, LANES)] = ot.T.astype(o_ref.dtype)
        return carry

    lax.fori_loop(0, n_groups, group, 0)


def _flash_attn(qt, kb, ak, vtb, ct, *, nh, tq):
    bsz, d, t = qt.shape
    dh = d // nh
    hpt = LANES // dh
    n_tiles = nh // hpt
    heads_per_step = FA_TILES * hpt
    assert n_tiles % FA_TILES == 0
    res3 = lambda shape: pl.BlockSpec(shape, lambda b, i: (b, 0, 0), pipeline_mode=pl.Buffered(1))
    return pl.pallas_call(
        functools.partial(_fa_kernel, tq=tq, dh=dh),
        grid=(bsz, t // tq),
        in_specs=[
            pl.BlockSpec((1, d, tq), lambda b, i: (b, 0, i)),
            res3((1, t, d)), res3((1, t, d)), res3((1, vtb.shape[1], t)),
            pl.BlockSpec((1, n_tiles, hpt, tq), lambda b, i: (b, 0, 0, i)),
        ],
        out_specs=pl.BlockSpec((1, tq, d), lambda b, i: (b, i, 0)),
        out_shape=jax.ShapeDtypeStruct((bsz, t, d), BF16),
        scratch_shapes=[pltpu.VMEM((heads_per_step, 1, tq), F32),
                        pltpu.VMEM((heads_per_step, dh + V_PAD, tq), F32),
                        pltpu.VMEM((heads_per_step, tq, tq), F32), pltpu.VMEM((heads_per_step, tq, tq), F32)],
        compiler_params=_cparams(2),
        name="flash_attn",
    )(qt, kb, ak, vtb, ct.reshape(bsz, n_tiles, hpt, t))


def _decode_kernel(pt_ref, q_ref, kn_ref, vn_ref, lfn_ref, *refs, pages, nh, dh):
    del pt_ref
    kt_refs = refs[:pages]
    vt_refs = refs[pages:2 * pages]
    lf_refs = refs[2 * pages:3 * pages]
    o_ref = refs[3 * pages]
    qbd_ref, m_ref, l_ref, acc_ref, r_ref = refs[3 * pages + 1:]
    c = pl.program_id(1)
    d = nh * dh
    page = kt_refs[0].shape[2]
    head_row = lax.broadcasted_iota(jnp.int32, (nh, d), 0)
    head_of_lane = lax.broadcasted_iota(jnp.int32, (nh, d), 1) // dh
    block_diag = head_of_lane == head_row

    @pl.when(c == 0)
    def _():
        qbd = jnp.where(block_diag, jnp.broadcast_to(q_ref[0], (nh, d)), 0.0)
        qbd_ref[...] = qbd.astype(BF16)
        m_ref[...] = jnp.sum(qbd * kn_ref[0], axis=-1, keepdims=True)
        l_ref[...] = jnp.ones_like(l_ref)
        acc_ref[...] = jnp.broadcast_to(vn_ref[0], (nh, d))
        eye = (lax.broadcasted_iota(jnp.int32, (nh, nh), 0)
               == lax.broadcasted_iota(jnp.int32, (nh, nh), 1))
        lfn = jnp.broadcast_to(lfn_ref[0], (nh, nh))
        r_ref[...] = jnp.sum(jnp.where(eye, lfn, 0.0), axis=-1, keepdims=True)

    qbd = qbd_ref[...]
    r = r_ref[...]
    key_lane = lax.broadcasted_iota(jnp.int32, (nh, page), 1)
    scores = [None] * pages
    for pg in reversed(range(pages)):
        s = jnp.dot(qbd, kt_refs[pg][0].astype(BF16), preferred_element_type=F32)
        lft = lf_refs[pg][0]
        inc = lft
        shift = 1
        while shift < page:
            moved = pltpu.roll(inc, page - shift, 1)
            inc = inc + jnp.where(key_lane < page - shift, moved, 0.0)
            shift *= 2
        scores[pg] = s + ((inc - lft) + r)
        r = r + inc[:, 0:1]
    r_ref[...] = r
    s_all = jnp.concatenate(scores, axis=1)
    m_old = m_ref[...]
    m_new = jnp.maximum(m_old, jnp.max(s_all, axis=-1, keepdims=True))
    a = jnp.exp(m_old - m_new)
    p = jnp.exp(s_all - m_new)
    l_ref[...] = a * l_ref[...] + jnp.sum(p, axis=-1, keepdims=True)
    p = p.astype(BF16)
    pv = None
    for pg in range(pages):
        part = lax.dot_general(p[:, pg * page:(pg + 1) * page], vt_refs[pg][0].astype(BF16), _NT,
                               preferred_element_type=F32)
        pv = part if pv is None else pv + part
    acc_ref[...] = a * acc_ref[...] + pv
    m_ref[...] = m_new

    @pl.when(c == pl.num_programs(1) - 1)
    def _():
        o_ref[0] = jnp.sum(jnp.where(block_diag, acc_ref[...] / l_ref[...], 0.0), axis=0, keepdims=True)


def _decode_attn(page_table, q, k_new, v_new, lf_new, cache_kt, cache_vt, cache_lft, *, nh, pages):
    n, d = q.shape
    dh = d // nh
    page = cache_lft.shape[2]
    n_chunks = page_table.shape[1] // pages

    def tok_spec(width):
        return pl.BlockSpec((1, 1, width), lambda b, c, pt: (b, 0, 0))

    def page_spec(rows, pg):
        return pl.BlockSpec(
            (1, rows, page),
            lambda b, c, pt: (pt[b, (n_chunks - 1 - c) * pages + pg], 0, 0))

    in_specs = [tok_spec(d), tok_spec(d), tok_spec(d), tok_spec(nh)]
    in_specs += [page_spec(d, pg) for pg in range(pages)]
    in_specs += [page_spec(d, pg) for pg in range(pages)]
    in_specs += [page_spec(nh, pg) for pg in range(pages)]
    grid_spec = pltpu.PrefetchScalarGridSpec(
        num_scalar_prefetch=1,
        grid=(n, n_chunks),
        in_specs=in_specs,
        out_specs=tok_spec(d),
        scratch_shapes=[
            pltpu.VMEM((nh, d), BF16), pltpu.VMEM((nh, 1), F32), pltpu.VMEM((nh, 1), F32),
            pltpu.VMEM((nh, d), F32), pltpu.VMEM((nh, 1), F32),
        ],
    )
    out = pl.pallas_call(
        functools.partial(_decode_kernel, pages=pages, nh=nh, dh=dh),
        grid_spec=grid_spec,
        out_shape=jax.ShapeDtypeStruct((n, 1, d), F32),
        compiler_params=_cparams(2),
        name="decode_attn",
    )(page_table, q.reshape(n, 1, d), k_new.reshape(n, 1, d), v_new.reshape(n, 1, d),
      lf_new.reshape(n, 1, nh), *([cache_kt] * pages), *([cache_vt] * pages), *([cache_lft] * pages))
    return out.reshape(n, d)


def _bias_placement(nh, dh, d):
    hpt = LANES // dh
    place = np.zeros((LANES, d), np.float32)
    for h in range(nh):
        col0 = (h // hpt) * LANES + AUG_PER_HEAD * (h % hpt)
        for piece in range(N_SPLIT):
            place[N_SPLIT * nh, col0 + piece] = 1.0
            place[piece * nh + h, col0 + N_SPLIT + piece] = -1.0
    return place


def kernel(x_prompt, x_sample, cache_k, cache_v, cache_logf, state_conv, page_table, a_w_in, a_conv_w, a_w_out, b_w_q, b_w_o, kv_w, f_w, f_b, ffn_w_gu, ffn_w_down, ln_g, ln_b):
    bsz, t, d = x_prompt.shape
    n_s = x_sample.shape[0]
    n_a, n_b = a_w_in.shape[0], b_w_q.shape[0]
    depth = n_a + n_b
    nh = f_w.shape[1]
    dh = d // nh
    n_pool, page = cache_logf.shape[0], cache_logf.shape[1]
    alpha = (2 * depth) ** 0.25
    attn_scale = dh ** -0.5
    q_scale = attn_scale * LOG2E
    tm_seq, tm_fused, tq, dec_pages = 512, 256, 512, 16
    assert LANES % dh == 0 and (N_SPLIT + 1) * nh <= LANES and AUG_PER_HEAD * (LANES // dh) <= LANES

    w_in = [a_w_in[l].astype(BF16) for l in range(n_a)]
    w_out = [a_w_out[l].astype(BF16) for l in range(n_a)]
    w_q = [b_w_q[j].astype(BF16) for j in range(n_b)]
    w_qt = [b_w_q[j].T.astype(BF16) for j in range(n_b)]
    w_o = [b_w_o[j].astype(BF16) for j in range(n_b)]
    w_kv = kv_w.astype(BF16)
    w_kvt = kv_w.T.astype(BF16)
    w_gu = [ffn_w_gu[l].astype(BF16) for l in range(depth)]
    w_down = [ffn_w_down[l].astype(BF16) for l in range(depth)]
    fw_pad = jnp.pad(f_w, ((0, 0), (0, LANES - nh))).astype(BF16)
    fb_pad = jnp.pad(f_b, (0, LANES - nh)).reshape(1, LANES)
    fw_rep = jnp.pad(jnp.tile(f_w, (1, N_SPLIT)), ((0, 0), (0, LANES - N_SPLIT * nh))).astype(BF16)
    fb_rep = jnp.pad(jnp.tile(f_b, N_SPLIT), (0, LANES - N_SPLIT * nh)).reshape(1, LANES)
    fwt = f_w.T.astype(BF16)
    fbt = f_b.reshape(nh, 1)
    place = jnp.asarray(_bias_placement(nh, dh, d), BF16)
    g = ln_g.reshape(depth, 2, 1, d)
    b = ln_b.reshape(depth, 2, 1, d)

    x = x_prompt
    hist0 = jnp.zeros((bsz, 2, d), F32)
    conv_p = []
    for l in range(n_a):
        x, st = _conv_seq(x, hist0, w_in[l], a_conv_w[l], w_out[l], g[l, 0], b[l, 0],
                          w_gu[l], w_down[l], g[l, 1], b[l, 1], alpha=alpha, tm=tm_fused)
        conv_p.append(st)
    kt_p, vt_p, lft_p, kb, ak, vtb, ct, qt = _kv_seq(x, w_kvt, w_kv[:, :d], fw_rep, fb_rep, fwt, fbt, place,
                                                     w_qt[0], nh=nh, tm=tm_seq, q_scale=q_scale)
    for j in range(n_b):
        l = n_a + j
        o = _flash_attn(qt, kb, ak, vtb, ct, nh=nh, tq=tq)
        w_next = w_qt[j + 1] if j + 1 < n_b else None
        out = _attn_ffn(o, x, w_o[j], g[l, 0], b[l, 0], w_gu[l], w_down[l], g[l, 1], b[l, 1], w_next,
                        alpha=alpha, tm=tm_fused, q_scale=q_scale)
        x, qt = out if w_next is not None else (out, None)
    y_prompt = x

    cache_kt = jnp.transpose(cache_k, (0, 2, 3, 1)).reshape(n_pool, d, page)
    cache_vt = jnp.transpose(cache_v, (0, 2, 3, 1)).reshape(n_pool, d, page)
    cache_lft = jnp.transpose(cache_logf, (0, 2, 1))
    xs = x_sample.reshape(n_s, d)
    conv_s = []
    for l in range(n_a):
        xs, st = _conv_tok(xs, state_conv[l].reshape(n_s, 2 * d), w_in[l], a_conv_w[l], w_out[l],
                           g[l, 0], b[l, 0], alpha=alpha)
        conv_s.append(st.reshape(n_s, 2, d))
        xs = _ffn(xs, w_gu[l], w_down[l], g[l, 1], b[l, 1], alpha=alpha, tm=n_s)
    k_s, v_s, lf_s = _kv_tok(xs, w_kv, fw_pad, fb_pad, nh=nh)
    for j in range(n_b):
        l = n_a + j
        qs = _proj(xs, w_q[j], scale=attn_scale, out_dtype=F32, tm=n_s)
        os_ = _decode_attn(page_table, qs, k_s, v_s, lf_s, cache_kt, cache_vt, cache_lft, nh=nh, pages=dec_pages)
        xs = _proj_ln(os_, xs, w_o[j], g[l, 0], b[l, 0], alpha=alpha, tm=n_s)
        xs = _ffn(xs, w_gu[l], w_down[l], g[l, 1], b[l, 1], alpha=alpha, tm=n_s)

    return (y_prompt, xs.reshape(n_s, 1, d),
            jnp.transpose(kt_p, (0, 3, 1, 2)), jnp.transpose(vt_p, (0, 3, 1, 2)), jnp.transpose(lft_p, (0, 2, 1)),
            jnp.stack(conv_p),
            k_s.reshape(n_s, 1, nh, dh), v_s.reshape(n_s, 1, nh, dh), lf_s.reshape(n_s, 1, nh),
            jnp.stack(conv_s))
```

```python
import functools
import math

import numpy as np
import jax
import jax.numpy as jnp
from jax import lax
from jax.experimental import pallas as pl
from jax.experimental.pallas import tpu as pltpu

F32 = jnp.float32
BF16 = jnp.bfloat16
LN_EPS = 1e-5
LOG2E = math.log2(math.e)
V7X_VMEM_LIMIT_BYTES = 56 * 1024 * 1024
LANES = 128
N_SPLIT = 3
AUG_PER_HEAD = 2 * N_SPLIT
V_PAD = 16
FA_TILES = 4

_NT = (((1,), (1,)), ((), ()))


def _cparams(n_axes):
    return pltpu.CompilerParams(
        dimension_semantics=("arbitrary",) * n_axes,
        vmem_limit_bytes=V7X_VMEM_LIMIT_BYTES,
    )


def _const_spec(shape):
    nd = len(shape)
    return pl.BlockSpec(shape, lambda *_: (0,) * nd, pipeline_mode=pl.Buffered(1))


class _Layer:
    def __init__(self, stack, index):
        self.stack, self.index = stack, index

    @property
    def shape(self):
        return self.stack.shape[1:]


def _resident_spec(c):
    if not isinstance(c, _Layer):
        return _const_spec(c.shape)
    nd, layer = c.stack.ndim, c.index
    return pl.BlockSpec((None,) + c.shape, lambda *_: (layer,) + (0,) * (nd - 1), pipeline_mode=pl.Buffered(1))


def _operand(c):
    return c.stack if isinstance(c, _Layer) else c


def _layer_norm(r, g, b):
    mu = jnp.mean(r, axis=-1, keepdims=True)
    c = r - mu
    var = jnp.mean(c * c, axis=-1, keepdims=True)
    return c * lax.rsqrt(var + LN_EPS) * g + b


def _log_sigmoid(z):
    return jnp.minimum(z, 0.0) - jnp.log1p(jnp.exp(-jnp.abs(z)))


def _split3(c):
    hi = c.astype(BF16).astype(F32)
    r = c - hi
    mid = r.astype(BF16).astype(F32)
    lo = (r - mid).astype(BF16).astype(F32)
    return hi, mid, lo


def _gated_conv_out(x, bch, p1, p2, cw_ref, w_out_ref, g_ref, b_ref, alpha):
    d = x.shape[-1]
    u = bch[:, d:2 * d] * bch[:, 2 * d:]
    z = cw_ref[2:3, :] * u
    z = z + cw_ref[0:1, :] * p2
    z = z + cw_ref[1:2, :] * p1
    y = jnp.dot((bch[:, :d] * z).astype(BF16), w_out_ref[...], preferred_element_type=F32)
    return _layer_norm(alpha * x + y, g_ref[...], b_ref[...])


def _ffn_value(x, wgu_ref, wd_ref, g_ref, b_ref, alpha):
    dff = wd_ref.shape[0]
    gu = jnp.dot(x.astype(BF16), wgu_ref[...], preferred_element_type=F32)
    gate = gu[:, :dff]
    a = (gate * jax.nn.sigmoid(gate)) * gu[:, dff:]
    y = jnp.dot(a.astype(BF16), wd_ref[...], preferred_element_type=F32)
    return _layer_norm(alpha * x + y, g_ref[...], b_ref[...])


def _q_transposed(x, wqt_ref, scale):
    y = lax.dot_general(wqt_ref[...], x.astype(BF16), _NT, preferred_element_type=F32)
    return (y * scale).astype(BF16)


def _conv_seq_kernel(x_ref, hist_ref, w_in_ref, cw_ref, w_out_ref, g_ref, b_ref, wgu_ref, wd_ref, g1_ref, b1_ref,
                     o_ref, st_ref, carry_ref, *, alpha):
    j = pl.program_id(1)

    @pl.when(j == 0)
    def _():
        carry_ref[...] = hist_ref[0]

    x = x_ref[0]
    tm, d = x.shape
    bch = jnp.dot(x.astype(BF16), w_in_ref[...], preferred_element_type=F32)
    u = bch[:, d:2 * d] * bch[:, 2 * d:]
    h0 = carry_ref[0:1, :]
    h1 = carry_ref[1:2, :]
    row = lax.broadcasted_iota(jnp.int32, (tm, d), 0)
    p1 = jnp.where(row == 0, h1, pltpu.roll(u, 1, 0))
    p2 = jnp.where(row == 0, h0, jnp.where(row == 1, h1, pltpu.roll(u, 2, 0)))
    tail = u[tm - 2:, :]
    carry_ref[...] = tail
    x1 = _gated_conv_out(x, bch, p1, p2, cw_ref, w_out_ref, g_ref, b_ref, alpha)
    o_ref[0] = _ffn_value(x1, wgu_ref, wd_ref, g1_ref, b1_ref, alpha)

    @pl.when(j == pl.num_programs(1) - 1)
    def _():
        st_ref[0] = tail


def _conv_seq(x, hist, w_in, cw, w_out, g, b, wgu, wd, g1, b1, *, alpha, tm):
    bsz, t, d = x.shape
    consts = (w_in, cw, w_out, g, b, wgu, wd, g1, b1)
    return pl.pallas_call(
        functools.partial(_conv_seq_kernel, alpha=alpha),
        grid=(bsz, t // tm),
        in_specs=[
            pl.BlockSpec((1, tm, d), lambda i, j: (i, j, 0)),
            pl.BlockSpec((1, 2, d), lambda i, j: (i, 0, 0)),
        ] + [_resident_spec(c) for c in consts],
        out_specs=[
            pl.BlockSpec((1, tm, d), lambda i, j: (i, j, 0)),
            pl.BlockSpec((1, 2, d), lambda i, j: (i, 0, 0)),
        ],
        out_shape=[jax.ShapeDtypeStruct((bsz, t, d), F32), jax.ShapeDtypeStruct((bsz, 2, d), F32)],
        scratch_shapes=[pltpu.VMEM((2, d), F32)],
        compiler_params=_cparams(2),
        name="conv_seq",
    )(x, hist, *map(_operand, consts))


def _conv_tok_kernel(x_ref, hist_ref, w_in_ref, cw_ref, w_out_ref, g_ref, b_ref,
                     o_ref, st_ref, *, alpha):
    x = x_ref[...]
    d = x.shape[-1]
    bch = jnp.dot(x.astype(BF16), w_in_ref[...], preferred_element_type=F32)
    u = bch[:, d:2 * d] * bch[:, 2 * d:]
    h0 = hist_ref[:, :d]
    h1 = hist_ref[:, d:]
    o_ref[...] = _gated_conv_out(x, bch, h1, h0, cw_ref, w_out_ref, g_ref, b_ref, alpha)
    st_ref[:, :d] = h1
    st_ref[:, d:] = u


def _conv_tok(x, hist, w_in, cw, w_out, g, b, *, alpha):
    n, d = x.shape
    args = (x, hist, w_in, cw, w_out, g, b)
    return pl.pallas_call(
        functools.partial(_conv_tok_kernel, alpha=alpha),
        grid=(1,),
        in_specs=[_resident_spec(a) for a in args],
        out_specs=[_const_spec((n, d)), _const_spec((n, 2 * d))],
        out_shape=[jax.ShapeDtypeStruct((n, d), F32), jax.ShapeDtypeStruct((n, 2 * d), F32)],
        compiler_params=_cparams(1),
        name="conv_tok",
    )(*map(_operand, args))


def _ffn_kernel(x_ref, wgu_ref, wd_ref, g_ref, b_ref, o_ref, *, alpha):
    o_ref[...] = _ffn_value(x_ref[...], wgu_ref, wd_ref, g_ref, b_ref, alpha)


def _ffn(x, wgu, wd, g, b, *, alpha, tm):
    m, d = x.shape
    return pl.pallas_call(
        functools.partial(_ffn_kernel, alpha=alpha),
        grid=(m // tm,),
        in_specs=[
            pl.BlockSpec((tm, d), lambda i: (i, 0)),
            _resident_spec(wgu), _resident_spec(wd), _const_spec(g.shape), _const_spec(b.shape),
        ],
        out_specs=pl.BlockSpec((tm, d), lambda i: (i, 0)),
        out_shape=jax.ShapeDtypeStruct((m, d), F32),
        compiler_params=_cparams(1),
        name="ffn",
    )(x, _operand(wgu), _operand(wd), g, b)


def _kv_seq_kernel(x_ref, wkvt_ref, wk_ref, fw_ref, fb_ref, fwt_ref, fbt_ref, place_ref, wqt_ref,
                   kt_ref, vt_ref, lft_ref, kb_ref, ak_ref, vtb_ref, ct_ref, qt_ref,
                   carry_ref, carry_t_ref, *, q_scale):
    j = pl.program_id(1)

    @pl.when(j == 0)
    def _():
        carry_ref[...] = jnp.zeros_like(carry_ref)
        carry_t_ref[...] = jnp.zeros_like(carry_t_ref)

    xb = x_ref[0].astype(BF16)
    tm, d = xb.shape
    nh = lft_ref.shape[1]
    kvt = lax.dot_general(wkvt_ref[...], xb, _NT, preferred_element_type=F32)
    kt_ref[0] = kvt[:d].reshape(kt_ref.shape[1:])
    vt3 = kvt[d:].reshape(vt_ref.shape[1:])
    vt_ref[0] = vt3
    ones = jnp.ones((nh, V_PAD, tm), F32)
    vtb_ref[0] = jnp.concatenate([vt3, ones], axis=1).reshape(vtb_ref.shape[1:]).astype(BF16)
    kb_ref[0] = jnp.dot(xb, wk_ref[...], preferred_element_type=F32).astype(BF16)
    qt_ref[0] = _q_transposed(xb, wqt_ref, q_scale)

    lft = _log_sigmoid(lax.dot_general(fwt_ref[...], xb, _NT, preferred_element_type=F32) + fbt_ref[...])
    lft_ref[0] = lft
    lane = lax.broadcasted_iota(jnp.int32, lft.shape, 1)
    ct = lft
    shift = 1
    while shift < tm:
        ct = ct + jnp.where(lane >= shift, pltpu.roll(ct, shift, 1), 0.0)
        shift *= 2
    ct = ct + carry_t_ref[...]
    carry_t_ref[...] = ct[:, tm - 1:]
    ct_ref[0] = ct * LOG2E

    lf = _log_sigmoid(jnp.dot(xb, fw_ref[...], preferred_element_type=F32) + fb_ref[...])
    row = lax.broadcasted_iota(jnp.int32, lf.shape, 0)
    c = lf
    shift = 1
    while shift < tm:
        c = c + jnp.where(row >= shift, pltpu.roll(c, shift, 0), 0.0)
        shift *= 2
    c = c + carry_ref[...]
    carry_ref[...] = c[tm - 1:, :]
    hi, mid, lo = _split3(c * LOG2E)
    grp = lax.broadcasted_iota(jnp.int32, lf.shape, 1) // nh
    pieces = jnp.where(grp == 0, hi, jnp.where(grp == 1, mid, jnp.where(grp == 2, lo,
                       jnp.where(grp == N_SPLIT, 1.0, 0.0))))
    ak_ref[0] = jnp.dot(pieces.astype(BF16), place_ref[...], preferred_element_type=F32).astype(BF16)


def _kv_seq(x, wkvt, wk, fw, fb, fwt, fbt, place, wqt, *, nh, tm, q_scale):
    bsz, t, d = x.shape
    dh = d // nh
    vrows = nh * (dh + V_PAD)
    tok = pl.BlockSpec((1, tm, d), lambda i, j: (i, j, 0))
    tr4 = pl.BlockSpec((1, nh, dh, tm), lambda i, j: (i, 0, 0, j))
    tr3 = pl.BlockSpec((1, nh, tm), lambda i, j: (i, 0, j))
    trv = pl.BlockSpec((1, vrows, tm), lambda i, j: (i, 0, j))
    trq = pl.BlockSpec((1, wqt.shape[0], tm), lambda i, j: (i, 0, j))
    consts = (wkvt, wk, fw, fb, fwt, fbt, place, wqt)
    return pl.pallas_call(
        functools.partial(_kv_seq_kernel, q_scale=q_scale),
        grid=(bsz, t // tm),
        in_specs=[tok] + [_resident_spec(a) for a in consts],
        out_specs=[tr4, tr4, tr3, tok, tok, trv, tr3, trq],
        out_shape=[
            jax.ShapeDtypeStruct((bsz, nh, dh, t), F32), jax.ShapeDtypeStruct((bsz, nh, dh, t), F32),
            jax.ShapeDtypeStruct((bsz, nh, t), F32),
            jax.ShapeDtypeStruct((bsz, t, d), BF16), jax.ShapeDtypeStruct((bsz, t, d), BF16),
            jax.ShapeDtypeStruct((bsz, vrows, t), BF16), jax.ShapeDtypeStruct((bsz, nh, t), F32),
            jax.ShapeDtypeStruct((bsz, wqt.shape[0], t), BF16),
        ],
        scratch_shapes=[pltpu.VMEM((1, fw.shape[1]), F32), pltpu.VMEM((nh, 1), F32)],
        compiler_params=_cparams(2),
        name="kv_seq",
    )(x, *map(_operand, consts))


def _kv_tok_kernel(x_ref, wkv_ref, fw_ref, fb_ref, k_ref, v_ref, lf_ref):
    xb = x_ref[...].astype(BF16)
    d = xb.shape[1]
    nh = lf_ref.shape[-1]
    kv = jnp.dot(xb, wkv_ref[...], preferred_element_type=F32)
    k_ref[...] = kv[:, :d]
    v_ref[...] = kv[:, d:]
    lf = _log_sigmoid(jnp.dot(xb, fw_ref[...], preferred_element_type=F32) + fb_ref[...])
    lf_ref[...] = lf[:, :nh]


def _kv_tok(x, wkv, fw, fb, *, nh):
    n, d = x.shape
    args = (x, wkv, fw, fb)
    return pl.pallas_call(
        _kv_tok_kernel,
        grid=(1,),
        in_specs=[_const_spec(a.shape) for a in args],
        out_specs=[_const_spec((n, d)), _const_spec((n, d)), _const_spec((n, nh))],
        out_shape=[jax.ShapeDtypeStruct((n, d), F32), jax.ShapeDtypeStruct((n, d), F32),
                   jax.ShapeDtypeStruct((n, nh), F32)],
        compiler_params=_cparams(1),
        name="kv_tok",
    )(*args)


def _proj_kernel(x_ref, w_ref, o_ref, *, scale):
    y = jnp.dot(x_ref[...].astype(BF16), w_ref[...], preferred_element_type=F32)
    o_ref[...] = (y * scale).astype(o_ref.dtype)


def _proj(x, w, *, scale, out_dtype, tm):
    m, d = x.shape
    n = w.shape[1]
    return pl.pallas_call(
        functools.partial(_proj_kernel, scale=scale),
        grid=(m // tm,),
        in_specs=[pl.BlockSpec((tm, d), lambda i: (i, 0)), _resident_spec(w)],
        out_specs=pl.BlockSpec((tm, n), lambda i: (i, 0)),
        out_shape=jax.ShapeDtypeStruct((m, n), out_dtype),
        compiler_params=_cparams(1),
        name="proj",
    )(x, _operand(w))


def _proj_ln_kernel(a_ref, x_ref, w_ref, g_ref, b_ref, o_ref, *, alpha):
    y = jnp.dot(a_ref[...].astype(BF16), w_ref[...], preferred_element_type=F32)
    o_ref[...] = _layer_norm(alpha * x_ref[...] + y, g_ref[...], b_ref[...])


def _proj_ln(a, x, w, g, b, *, alpha, tm):
    m, d = x.shape
    row = lambda i: (i, 0)
    return pl.pallas_call(
        functools.partial(_proj_ln_kernel, alpha=alpha),
        grid=(m // tm,),
        in_specs=[pl.BlockSpec((tm, a.shape[1]), row), pl.BlockSpec((tm, d), row),
                  _resident_spec(w), _const_spec(g.shape), _const_spec(b.shape)],
        out_specs=pl.BlockSpec((tm, d), row),
        out_shape=jax.ShapeDtypeStruct((m, d), F32),
        compiler_params=_cparams(1),
        name="proj_ln",
    )(a, x, _operand(w), g, b)


def _attn_ffn_kernel(a_ref, x_ref, w_ref, g_ref, b_ref, wgu_ref, wd_ref, g1_ref, b1_ref, *refs, alpha, q_scale):
    y = jnp.dot(a_ref[0], w_ref[...], preferred_element_type=F32)
    x1 = _layer_norm(alpha * x_ref[0] + y, g_ref[...], b_ref[...])
    x2 = _ffn_value(x1, wgu_ref, wd_ref, g1_ref, b1_ref, alpha)
    if len(refs) == 1:
        (o_ref,) = refs
    else:
        wqt_ref, o_ref, qt_ref = refs
        qt_ref[0] = _q_transposed(x2, wqt_ref, q_scale)
    o_ref[0] = x2


def _attn_ffn(a, x, w, g, b, wgu, wd, g1, b1, wqt, *, alpha, tm, q_scale):
    bsz, t, d = x.shape
    tok = pl.BlockSpec((1, tm, d), lambda i, j: (i, j, 0))
    consts = (w, g, b, wgu, wd, g1, b1) + (() if wqt is None else (wqt,))
    out_specs, out_shape = [tok], [jax.ShapeDtypeStruct((bsz, t, d), F32)]
    if wqt is not None:
        out_specs.append(pl.BlockSpec((1, wqt.shape[0], tm), lambda i, j: (i, 0, j)))
        out_shape.append(jax.ShapeDtypeStruct((bsz, wqt.shape[0], t), BF16))
    out = pl.pallas_call(
        functools.partial(_attn_ffn_kernel, alpha=alpha, q_scale=q_scale),
        grid=(bsz, t // tm),
        in_specs=[tok, tok] + [_resident_spec(c) for c in consts],
        out_specs=out_specs,
        out_shape=out_shape,
        compiler_params=_cparams(2),
        name="attn_ffn",
    )(a, x, *map(_operand, consts))
    return out[0] if wqt is None else out


def _fa_kernel(qt_ref, k_ref, ak_ref, vt_ref, c_ref, o_ref, m_ref, acc_ref, st_a_ref, st_b_ref, *, tq, dh):
    i = pl.program_id(1)
    hpt = LANES // dh
    vrows = dh + V_PAD
    n_groups = k_ref.shape[2] // (LANES * FA_TILES)
    sub = lax.broadcasted_iota(jnp.int32, (LANES, tq), 0)
    key_idx = lax.broadcasted_iota(jnp.int32, (tq, tq), 0)
    qry_idx = lax.broadcasted_iota(jnp.int32, (tq, tq), 1)

    def group(gi, carry):
        tiles = [gi * FA_TILES + lt for lt in range(FA_TILES)]
        los = [pl.multiple_of(tile * LANES, LANES) for tile in tiles]
        qp = []
        for tile, lo in zip(tiles, los):
            q2 = qt_ref[0, pl.ds(lo, LANES), :].astype(F32)
            c2 = c_ref[0, tile]
            for jh in range(hpt):
                qm = jnp.where((sub >= jh * dh) & (sub < (jh + 1) * dh), q2, 0.0)
                hi, mid, lw = _split3(c2[jh:jh + 1, :])
                base = AUG_PER_HEAD * jh
                aug = jnp.where(sub == base, hi, jnp.where(sub == base + 1, mid, jnp.where(sub == base + 2, lw,
                                jnp.where((sub >= base + N_SPLIT) & (sub < base + AUG_PER_HEAD), 1.0, 0.0))))
                qp.append(jnp.concatenate([qm, aug], axis=0).astype(BF16))
        m_ref[...] = jnp.full_like(m_ref, -jnp.inf)
        acc_ref[...] = jnp.zeros_like(acc_ref)

        def scores(kb, st_ref):
            ks = pl.multiple_of(kb * tq, tq)
            for lt, lo in enumerate(los):
                kk = jnp.concatenate([k_ref[0, pl.ds(ks, tq), pl.ds(lo, LANES)],
                                      ak_ref[0, pl.ds(ks, tq), pl.ds(lo, LANES)]], axis=1)
                for jh in range(hpt):
                    st_ref[lt * hpt + jh] = jnp.dot(kk, qp[lt * hpt + jh], preferred_element_type=F32)

        def accumulate(kb, st_ref, masked):
            ks = pl.multiple_of(kb * tq, tq)
            for lt, tile in enumerate(tiles):
                v2 = vt_ref[0, pl.ds(pl.multiple_of(tile * (hpt * vrows), hpt * vrows), hpt * vrows), pl.ds(ks, tq)]
                for jh in range(hpt):
                    hh = lt * hpt + jh
                    st = st_ref[hh]
                    if masked:
                        st = jnp.where(key_idx <= qry_idx, st, -jnp.inf)
                    m_old = m_ref[hh]
                    m_new = jnp.maximum(m_old, jnp.max(st, axis=0, keepdims=True))
                    p = jnp.exp2(st - m_new).astype(BF16)
                    pv = jnp.dot(v2[jh * vrows:(jh + 1) * vrows, :], p, preferred_element_type=F32)
                    acc_ref[hh] = jnp.exp2(m_old - m_new) * acc_ref[hh] + pv
                    m_ref[hh] = m_new

        def two_steps(tt, c):
            kb = 2 * tt
            scores(kb + 1, st_b_ref)
            accumulate(kb, st_a_ref, False)
            scores(kb + 2, st_a_ref)
            accumulate(kb + 1, st_b_ref, False)
            return c

        scores(0, st_a_ref)
        lax.fori_loop(0, i // 2, two_steps, 0)

        @pl.when(i % 2 == 1)
        def _():
            scores(i, st_b_ref)
            accumulate(i - 1, st_a_ref, False)
            accumulate(i, st_b_ref, True)

        @pl.when(i % 2 == 0)
        def _():
            accumulate(i, st_a_ref, True)

        for lt, lo in enumerate(los):
            heads = [acc_ref[lt * hpt + jh] for jh in range(hpt)]
            ot = jnp.concatenate([acc[:dh] / acc[dh:dh + 1] for acc in heads], axis=0)
            o_ref[0, :, pl.ds(lo, LANES)] = ot.T.astype(o_ref.dtype)
        return carry

    lax.fori_loop(0, n_groups, group, 0)


def _flash_attn(qt, kb, ak, vtb, ct, *, nh, tq):
    bsz, d, t = qt.shape
    dh = d // nh
    hpt = LANES // dh
    n_tiles = nh // hpt
    heads_per_step = FA_TILES * hpt
    assert n_tiles % FA_TILES == 0
    res3 = lambda shape: pl.BlockSpec(shape, lambda b, i: (b, 0, 0), pipeline_mode=pl.Buffered(1))
    return pl.pallas_call(
        functools.partial(_fa_kernel, tq=tq, dh=dh),
        grid=(bsz, t // tq),
        in_specs=[
            pl.BlockSpec((1, d, tq), lambda b, i: (b, 0, i)),
            res3((1, t, d)), res3((1, t, d)), res3((1, vtb.shape[1], t)),
            pl.BlockSpec((1, n_tiles, hpt, tq), lambda b, i: (b, 0, 0, i)),
        ],
        out_specs=pl.BlockSpec((1, tq, d), lambda b, i: (b, i, 0)),
        out_shape=jax.ShapeDtypeStruct((bsz, t, d), BF16),
        scratch_shapes=[pltpu.VMEM((heads_per_step, 1, tq), F32),
                        pltpu.VMEM((heads_per_step, dh + V_PAD, tq), F32),
                        pltpu.VMEM((heads_per_step, tq, tq), F32), pltpu.VMEM((heads_per_step, tq, tq), F32)],
        compiler_params=_cparams(2),
        name="flash_attn",
    )(qt, kb, ak, vtb, ct.reshape(bsz, n_tiles, hpt, t))


def _decode_kernel(pt_ref, q_ref, kn_ref, vn_ref, lfn_ref, *refs, pages, nh, dh):
    del pt_ref
    kt_refs = refs[:pages]
    vt_refs = refs[pages:2 * pages]
    lf_refs = refs[2 * pages:3 * pages]
    o_ref = refs[3 * pages]
    qbd_ref, m_ref, l_ref, acc_ref, r_ref = refs[3 * pages + 1:]
    c = pl.program_id(1)
    d = nh * dh
    page = kt_refs[0].shape[2]
    head_row = lax.broadcasted_iota(jnp.int32, (nh, d), 0)
    head_of_lane = lax.broadcasted_iota(jnp.int32, (nh, d), 1) // dh
    block_diag = head_of_lane == head_row

    @pl.when(c == 0)
    def _():
        qbd = jnp.where(block_diag, jnp.broadcast_to(q_ref[0], (nh, d)), 0.0)
        qbd_ref[...] = qbd.astype(BF16)
        m_ref[...] = jnp.sum(qbd * kn_ref[0], axis=-1, keepdims=True)
        l_ref[...] = jnp.ones_like(l_ref)
        acc_ref[...] = jnp.broadcast_to(vn_ref[0], (nh, d))
        eye = (lax.broadcasted_iota(jnp.int32, (nh, nh), 0)
               == lax.broadcasted_iota(jnp.int32, (nh, nh), 1))
        lfn = jnp.broadcast_to(lfn_ref[0], (nh, nh))
        r_ref[...] = jnp.sum(jnp.where(eye, lfn, 0.0), axis=-1, keepdims=True)

    qbd = qbd_ref[...]
    r = r_ref[...]
    key_lane = lax.broadcasted_iota(jnp.int32, (nh, page), 1)
    scores = [None] * pages
    for pg in reversed(range(pages)):
        s = jnp.dot(qbd, kt_refs[pg][0].astype(BF16), preferred_element_type=F32)
        lft = lf_refs[pg][0]
        inc = lft
        shift = 1
        while shift < page:
            moved = pltpu.roll(inc, page - shift, 1)
            inc = inc + jnp.where(key_lane < page - shift, moved, 0.0)
            shift *= 2
        scores[pg] = s + ((inc - lft) + r)
        r = r + inc[:, 0:1]
    r_ref[...] = r
    s_all = jnp.concatenate(scores, axis=1)
    m_old = m_ref[...]
    m_new = jnp.maximum(m_old, jnp.max(s_all, axis=-1, keepdims=True))
    a = jnp.exp(m_old - m_new)
    p = jnp.exp(s_all - m_new)
    l_ref[...] = a * l_ref[...] + jnp.sum(p, axis=-1, keepdims=True)
    p = p.astype(BF16)
    pv = None
    for pg in range(pages):
        part = lax.dot_general(p[:, pg * page:(pg + 1) * page], vt_refs[pg][0].astype(BF16), _NT,
                               preferred_element_type=F32)
        pv = part if pv is None else pv + part
    acc_ref[...] = a * acc_ref[...] + pv
    m_ref[...] = m_new

    @pl.when(c == pl.num_programs(1) - 1)
    def _():
        o_ref[0] = jnp.sum(jnp.where(block_diag, acc_ref[...] / l_ref[...], 0.0), axis=0, keepdims=True)


def _decode_attn(page_table, q, k_new, v_new, lf_new, cache_kt, cache_vt, cache_lft, *, nh, pages):
    n, d = q.shape
    dh = d // nh
    page = cache_lft.shape[2]
    n_chunks = page_table.shape[1] // pages

    def tok_spec(width):
        return pl.BlockSpec((1, 1, width), lambda b, c, pt: (b, 0, 0))

    def page_spec(rows, pg):
        return pl.BlockSpec(
            (1, rows, page),
            lambda b, c, pt: (pt[b, (n_chunks - 1 - c) * pages + pg], 0, 0))

    in_specs = [tok_spec(d), tok_spec(d), tok_spec(d), tok_spec(nh)]
    in_specs += [page_spec(d, pg) for pg in range(pages)]
    in_specs += [page_spec(d, pg) for pg in range(pages)]
    in_specs += [page_spec(nh, pg) for pg in range(pages)]
    grid_spec = pltpu.PrefetchScalarGridSpec(
        num_scalar_prefetch=1,
        grid=(n, n_chunks),
        in_specs=in_specs,
        out_specs=tok_spec(d),
        scratch_shapes=[
            pltpu.VMEM((nh, d), BF16), pltpu.VMEM((nh, 1), F32), pltpu.VMEM((nh, 1), F32),
            pltpu.VMEM((nh, d), F32), pltpu.VMEM((nh, 1), F32),
        ],
    )
    out = pl.pallas_call(
        functools.partial(_decode_kernel, pages=pages, nh=nh, dh=dh),
        grid_spec=grid_spec,
        out_shape=jax.ShapeDtypeStruct((n, 1, d), F32),
        compiler_params=_cparams(2),
        name="decode_attn",
    )(page_table, q.reshape(n, 1, d), k_new.reshape(n, 1, d), v_new.reshape(n, 1, d),
      lf_new.reshape(n, 1, nh), *([cache_kt] * pages), *([cache_vt] * pages), *([cache_lft] * pages))
    return out.reshape(n, d)


def _bias_placement(nh, dh, d):
    hpt = LANES // dh
    place = np.zeros((LANES, d), np.float32)
    for h in range(nh):
        col0 = (h // hpt) * LANES + AUG_PER_HEAD * (h % hpt)
        for piece in range(N_SPLIT):
            place[N_SPLIT * nh, col0 + piece] = 1.0
            place[piece * nh + h, col0 + N_SPLIT + piece] = -1.0
    return place


def kernel(x_prompt, x_sample, cache_k, cache_v, cache_logf, state_conv, page_table, a_w_in, a_conv_w, a_w_out, b_w_q, b_w_o, kv_w, f_w, f_b, ffn_w_gu, ffn_w_down, ln_g, ln_b):
    bsz, t, d = x_prompt.shape
    n_s = x_sample.shape[0]
    n_a, n_b = a_w_in.shape[0], b_w_q.shape[0]
    depth = n_a + n_b
    nh = f_w.shape[1]
    dh = d // nh
    n_pool, page = cache_logf.shape[0], cache_logf.shape[1]
    alpha = (2 * depth) ** 0.25
    attn_scale = dh ** -0.5
    q_scale = attn_scale * LOG2E
    tm_seq, tm_fused, tq, dec_pages = 512, 256, 512, 16
    assert LANES % dh == 0 and (N_SPLIT + 1) * nh <= LANES and AUG_PER_HEAD * (LANES // dh) <= LANES

    def layers(stack):
        return [_Layer(stack, l) for l in range(stack.shape[0])]

    w_in, w_out = layers(a_w_in.astype(BF16)), layers(a_w_out.astype(BF16))
    w_q, w_o = layers(b_w_q.astype(BF16)), layers(b_w_o.astype(BF16))
    w_qt = layers(jnp.swapaxes(b_w_q, 1, 2).astype(BF16))
    w_kv = kv_w.astype(BF16)
    w_kvt = kv_w.T.astype(BF16)
    w_gu, w_down = layers(ffn_w_gu.astype(BF16)), layers(ffn_w_down.astype(BF16))
    fw_pad = jnp.pad(f_w, ((0, 0), (0, LANES - nh))).astype(BF16)
    fb_pad = jnp.pad(f_b, (0, LANES - nh)).reshape(1, LANES)
    fw_rep = jnp.pad(jnp.tile(f_w, (1, N_SPLIT)), ((0, 0), (0, LANES - N_SPLIT * nh))).astype(BF16)
    fb_rep = jnp.pad(jnp.tile(f_b, N_SPLIT), (0, LANES - N_SPLIT * nh)).reshape(1, LANES)
    fwt = f_w.T.astype(BF16)
    fbt = f_b.reshape(nh, 1)
    place = jnp.asarray(_bias_placement(nh, dh, d), BF16)
    g = ln_g.reshape(depth, 2, 1, d)
    b = ln_b.reshape(depth, 2, 1, d)

    x = x_prompt
    hist0 = jnp.zeros((bsz, 2, d), F32)
    conv_p = []
    for l in range(n_a):
        x, st = _conv_seq(x, hist0, w_in[l], a_conv_w[l], w_out[l], g[l, 0], b[l, 0],
                          w_gu[l], w_down[l], g[l, 1], b[l, 1], alpha=alpha, tm=tm_fused)
        conv_p.append(st)
    kt_p, vt_p, lft_p, kb, ak, vtb, ct, qt = _kv_seq(x, w_kvt, w_kv[:, :d], fw_rep, fb_rep, fwt, fbt, place,
                                                     w_qt[0], nh=nh, tm=tm_seq, q_scale=q_scale)
    for j in range(n_b):
        l = n_a + j
        o = _flash_attn(qt, kb, ak, vtb, ct, nh=nh, tq=tq)
        w_next = w_qt[j + 1] if j + 1 < n_b else None
        out = _attn_ffn(o, x, w_o[j], g[l, 0], b[l, 0], w_gu[l], w_down[l], g[l, 1], b[l, 1], w_next,
                        alpha=alpha, tm=tm_fused, q_scale=q_scale)
        x, qt = out if w_next is not None else (out, None)
    y_prompt = x

    cache_kt = jnp.transpose(cache_k, (0, 2, 3, 1)).reshape(n_pool, d, page)
    cache_vt = jnp.transpose(cache_v, (0, 2, 3, 1)).reshape(n_pool, d, page)
    cache_lft = jnp.transpose(cache_logf, (0, 2, 1))
    xs = x_sample.reshape(n_s, d)
    conv_s = []
    for l in range(n_a):
        xs, st = _conv_tok(xs, state_conv[l].reshape(n_s, 2 * d), w_in[l], a_conv_w[l], w_out[l],
                           g[l, 0], b[l, 0], alpha=alpha)
        conv_s.append(st.reshape(n_s, 2, d))
        xs = _ffn(xs, w_gu[l], w_down[l], g[l, 1], b[l, 1], alpha=alpha, tm=n_s)
    k_s, v_s, lf_s = _kv_tok(xs, w_kv, fw_pad, fb_pad, nh=nh)
    for j in range(n_b):
        l = n_a + j
        qs = _proj(xs, w_q[j], scale=attn_scale, out_dtype=F32, tm=n_s)
        os_ = _decode_attn(page_table, qs, k_s, v_s, lf_s, cache_kt, cache_vt, cache_lft, nh=nh, pages=dec_pages)
        xs = _proj_ln(os_, xs, w_o[j], g[l, 0], b[l, 0], alpha=alpha, tm=n_s)
        xs = _ffn(xs, w_gu[l], w_down[l], g[l, 1], b[l, 1], alpha=alpha, tm=n_s)

    return (y_prompt, xs.reshape(n_s, 1, d),
            jnp.transpose(kt_p, (0, 3, 1, 2)), jnp.transpose(vt_p, (0, 3, 1, 2)), jnp.transpose(lft_p, (0, 2, 1)),
            jnp.stack(conv_p),
            k_s.reshape(n_s, 1, nh, dh), v_s.reshape(n_s, 1, nh, dh), lf_s.reshape(n_s, 1, nh),
            jnp.stack(conv_s))
```

```python
import functools
import math

import numpy as np
import jax
import jax.numpy as jnp
from jax import lax
from jax.experimental import pallas as pl
from jax.experimental.pallas import tpu as pltpu

F32 = jnp.float32
BF16 = jnp.bfloat16
LN_EPS = 1e-5
LOG2E = math.log2(math.e)
V7X_VMEM_LIMIT_BYTES = 56 * 1024 * 1024
LANES = 128
N_SPLIT = 3
AUG_PER_HEAD = 2 * N_SPLIT
V_PAD = 16
FA_TILES = 4

_NT = (((1,), (1,)), ((), ()))


def _cparams(n_axes):
    return pltpu.CompilerParams(
        dimension_semantics=("arbitrary",) * n_axes,
        vmem_limit_bytes=V7X_VMEM_LIMIT_BYTES,
    )


def _const_spec(shape):
    nd = len(shape)
    return pl.BlockSpec(shape, lambda *_: (0,) * nd, pipeline_mode=pl.Buffered(1))


class _Layer:
    def __init__(self, stack, index):
        self.stack, self.index = stack, index

    @property
    def shape(self):
        return self.stack.shape[1:]


def _resident_spec(c):
    if not isinstance(c, _Layer):
        return _const_spec(c.shape)
    nd, layer = c.stack.ndim, c.index
    return pl.BlockSpec((None,) + c.shape, lambda *_: (layer,) + (0,) * (nd - 1), pipeline_mode=pl.Buffered(1))


def _operand(c):
    return c.stack if isinstance(c, _Layer) else c


def _layer_norm(r, g, b):
    mu = jnp.mean(r, axis=-1, keepdims=True)
    c = r - mu
    var = jnp.mean(c * c, axis=-1, keepdims=True)
    return c * lax.rsqrt(var + LN_EPS) * g + b


def _log_sigmoid(z):
    return jnp.minimum(z, 0.0) - jnp.log1p(jnp.exp(-jnp.abs(z)))


def _split3(c):
    hi = c.astype(BF16).astype(F32)
    r = c - hi
    mid = r.astype(BF16).astype(F32)
    lo = (r - mid).astype(BF16).astype(F32)
    return hi, mid, lo


def _gated_conv_out(x, bch, p1, p2, cw_ref, w_out_ref, g_ref, b_ref, alpha):
    d = x.shape[-1]
    u = bch[:, d:2 * d] * bch[:, 2 * d:]
    z = cw_ref[2:3, :] * u
    z = z + cw_ref[0:1, :] * p2
    z = z + cw_ref[1:2, :] * p1
    y = jnp.dot((bch[:, :d] * z).astype(BF16), w_out_ref[...], preferred_element_type=F32)
    return _layer_norm(alpha * x + y, g_ref[...], b_ref[...])


def _ffn_value(x, wgu_ref, wd_ref, g_ref, b_ref, alpha):
    dff = wd_ref.shape[0]
    gu = jnp.dot(x.astype(BF16), wgu_ref[...], preferred_element_type=F32)
    gate = gu[:, :dff]
    a = (gate * jax.nn.sigmoid(gate)) * gu[:, dff:]
    y = jnp.dot(a.astype(BF16), wd_ref[...], preferred_element_type=F32)
    return _layer_norm(alpha * x + y, g_ref[...], b_ref[...])


def _q_transposed(x, wqt_ref, scale):
    y = lax.dot_general(wqt_ref[...], x.astype(BF16), _NT, preferred_element_type=F32)
    return (y * scale).astype(BF16)


def _conv_seq_kernel(x_ref, hist_ref, w_in_ref, cw_ref, w_out_ref, g_ref, b_ref, wgu_ref, wd_ref, g1_ref, b1_ref,
                     o_ref, st_ref, carry_ref, *, alpha):
    j = pl.program_id(1)

    @pl.when(j == 0)
    def _():
        carry_ref[...] = hist_ref[0]

    x = x_ref[0]
    tm, d = x.shape
    bch = jnp.dot(x.astype(BF16), w_in_ref[...], preferred_element_type=F32)
    u = bch[:, d:2 * d] * bch[:, 2 * d:]
    h0 = carry_ref[0:1, :]
    h1 = carry_ref[1:2, :]
    row = lax.broadcasted_iota(jnp.int32, (tm, d), 0)
    p1 = jnp.where(row == 0, h1, pltpu.roll(u, 1, 0))
    p2 = jnp.where(row == 0, h0, jnp.where(row == 1, h1, pltpu.roll(u, 2, 0)))
    tail = u[tm - 2:, :]
    carry_ref[...] = tail
    x1 = _gated_conv_out(x, bch, p1, p2, cw_ref, w_out_ref, g_ref, b_ref, alpha)
    o_ref[0] = _ffn_value(x1, wgu_ref, wd_ref, g1_ref, b1_ref, alpha)

    @pl.when(j == pl.num_programs(1) - 1)
    def _():
        st_ref[0] = tail


def _conv_seq(x, hist, w_in, cw, w_out, g, b, wgu, wd, g1, b1, *, alpha, tm):
    bsz, t, d = x.shape
    consts = (w_in, cw, w_out, g, b, wgu, wd, g1, b1)
    return pl.pallas_call(
        functools.partial(_conv_seq_kernel, alpha=alpha),
        grid=(bsz, t // tm),
        in_specs=[
            pl.BlockSpec((1, tm, d), lambda i, j: (i, j, 0)),
            pl.BlockSpec((1, 2, d), lambda i, j: (i, 0, 0)),
        ] + [_resident_spec(c) for c in consts],
        out_specs=[
            pl.BlockSpec((1, tm, d), lambda i, j: (i, j, 0)),
            pl.BlockSpec((1, 2, d), lambda i, j: (i, 0, 0)),
        ],
        out_shape=[jax.ShapeDtypeStruct((bsz, t, d), F32), jax.ShapeDtypeStruct((bsz, 2, d), F32)],
        scratch_shapes=[pltpu.VMEM((2, d), F32)],
        compiler_params=_cparams(2),
        name="conv_seq",
    )(x, hist, *map(_operand, consts))


def _conv_tok_kernel(x_ref, hist_ref, w_in_ref, cw_ref, w_out_ref, g_ref, b_ref,
                     o_ref, st_ref, *, alpha):
    x = x_ref[...]
    d = x.shape[-1]
    bch = jnp.dot(x.astype(BF16), w_in_ref[...], preferred_element_type=F32)
    u = bch[:, d:2 * d] * bch[:, 2 * d:]
    h0 = hist_ref[:, :d]
    h1 = hist_ref[:, d:]
    o_ref[...] = _gated_conv_out(x, bch, h1, h0, cw_ref, w_out_ref, g_ref, b_ref, alpha)
    st_ref[:, :d] = h1
    st_ref[:, d:] = u


def _conv_tok(x, hist, w_in, cw, w_out, g, b, *, alpha):
    n, d = x.shape
    args = (x, hist, w_in, cw, w_out, g, b)
    return pl.pallas_call(
        functools.partial(_conv_tok_kernel, alpha=alpha),
        grid=(1,),
        in_specs=[_resident_spec(a) for a in args],
        out_specs=[_const_spec((n, d)), _const_spec((n, 2 * d))],
        out_shape=[jax.ShapeDtypeStruct((n, d), F32), jax.ShapeDtypeStruct((n, 2 * d), F32)],
        compiler_params=_cparams(1),
        name="conv_tok",
    )(*map(_operand, args))


def _ffn_kernel(x_ref, wgu_ref, wd_ref, g_ref, b_ref, o_ref, *, alpha):
    o_ref[...] = _ffn_value(x_ref[...], wgu_ref, wd_ref, g_ref, b_ref, alpha)


def _ffn(x, wgu, wd, g, b, *, alpha, tm):
    m, d = x.shape
    return pl.pallas_call(
        functools.partial(_ffn_kernel, alpha=alpha),
        grid=(m // tm,),
        in_specs=[
            pl.BlockSpec((tm, d), lambda i: (i, 0)),
            _resident_spec(wgu), _resident_spec(wd), _const_spec(g.shape), _const_spec(b.shape),
        ],
        out_specs=pl.BlockSpec((tm, d), lambda i: (i, 0)),
        out_shape=jax.ShapeDtypeStruct((m, d), F32),
        compiler_params=_cparams(1),
        name="ffn",
    )(x, _operand(wgu), _operand(wd), g, b)


def _kv_seq_kernel(x_ref, wkvt_ref, fw_ref, fb_ref, fwt_ref, fbt_ref, place_ref, wqt_ref,
                   kt_ref, vt_ref, lft_ref, kb_ref, ak_ref, vtb_ref, ct_ref, qt_ref,
                   carry_ref, carry_t_ref, *, q_scale):
    j = pl.program_id(1)

    @pl.when(j == 0)
    def _():
        carry_ref[...] = jnp.zeros_like(carry_ref)
        carry_t_ref[...] = jnp.zeros_like(carry_t_ref)

    xb = x_ref[0].astype(BF16)
    tm, d = xb.shape
    nh = lft_ref.shape[1]
    kvt = lax.dot_general(wkvt_ref[...], xb, _NT, preferred_element_type=F32)
    kt_ref[0] = kvt[:d].reshape(kt_ref.shape[1:])
    vt3 = kvt[d:].reshape(vt_ref.shape[1:])
    vt_ref[0] = vt3
    ones = jnp.ones((nh, V_PAD, tm), F32)
    vtb_ref[0] = jnp.concatenate([vt3, ones], axis=1).reshape(vtb_ref.shape[1:]).astype(BF16)
    kb_ref[0] = kvt[:d].T.astype(BF16)
    qt_ref[0] = _q_transposed(xb, wqt_ref, q_scale)

    lft = _log_sigmoid(lax.dot_general(fwt_ref[...], xb, _NT, preferred_element_type=F32) + fbt_ref[...])
    lft_ref[0] = lft
    lane = lax.broadcasted_iota(jnp.int32, lft.shape, 1)
    ct = lft
    shift = 1
    while shift < tm:
        ct = ct + jnp.where(lane >= shift, pltpu.roll(ct, shift, 1), 0.0)
        shift *= 2
    ct = ct + carry_t_ref[...]
    carry_t_ref[...] = ct[:, tm - 1:]
    ct_ref[0] = ct * LOG2E

    lf = _log_sigmoid(jnp.dot(xb, fw_ref[...], preferred_element_type=F32) + fb_ref[...])
    row = lax.broadcasted_iota(jnp.int32, lf.shape, 0)
    c = lf
    shift = 1
    while shift < tm:
        c = c + jnp.where(row >= shift, pltpu.roll(c, shift, 0), 0.0)
        shift *= 2
    c = c + carry_ref[...]
    carry_ref[...] = c[tm - 1:, :]
    hi, mid, lo = _split3(c * LOG2E)
    grp = lax.broadcasted_iota(jnp.int32, lf.shape, 1) // nh
    pieces = jnp.where(grp == 0, hi, jnp.where(grp == 1, mid, jnp.where(grp == 2, lo,
                       jnp.where(grp == N_SPLIT, 1.0, 0.0))))
    ak_ref[0] = jnp.dot(pieces.astype(BF16), place_ref[...], preferred_element_type=F32).astype(BF16)


def _kv_seq(x, wkvt, fw, fb, fwt, fbt, place, wqt, *, nh, tm, q_scale):
    bsz, t, d = x.shape
    dh = d // nh
    vrows = nh * (dh + V_PAD)
    tok = pl.BlockSpec((1, tm, d), lambda i, j: (i, j, 0))
    tr4 = pl.BlockSpec((1, nh, dh, tm), lambda i, j: (i, 0, 0, j))
    tr3 = pl.BlockSpec((1, nh, tm), lambda i, j: (i, 0, j))
    trv = pl.BlockSpec((1, vrows, tm), lambda i, j: (i, 0, j))
    trq = pl.BlockSpec((1, wqt.shape[0], tm), lambda i, j: (i, 0, j))
    consts = (wkvt, fw, fb, fwt, fbt, place, wqt)
    return pl.pallas_call(
        functools.partial(_kv_seq_kernel, q_scale=q_scale),
        grid=(bsz, t // tm),
        in_specs=[tok] + [_resident_spec(a) for a in consts],
        out_specs=[tr4, tr4, tr3, tok, tok, trv, tr3, trq],
        out_shape=[
            jax.ShapeDtypeStruct((bsz, nh, dh, t), F32), jax.ShapeDtypeStruct((bsz, nh, dh, t), F32),
            jax.ShapeDtypeStruct((bsz, nh, t), F32),
            jax.ShapeDtypeStruct((bsz, t, d), BF16), jax.ShapeDtypeStruct((bsz, t, d), BF16),
            jax.ShapeDtypeStruct((bsz, vrows, t), BF16), jax.ShapeDtypeStruct((bsz, nh, t), F32),
            jax.ShapeDtypeStruct((bsz, wqt.shape[0], t), BF16),
        ],
        scratch_shapes=[pltpu.VMEM((1, fw.shape[1]), F32), pltpu.VMEM((nh, 1), F32)],
        compiler_params=_cparams(2),
        name="kv_seq",
    )(x, *map(_operand, consts))


def _kv_tok_kernel(x_ref, wkv_ref, fw_ref, fb_ref, k_ref, v_ref, lf_ref):
    xb = x_ref[...].astype(BF16)
    d = xb.shape[1]
    nh = lf_ref.shape[-1]
    kv = jnp.dot(xb, wkv_ref[...], preferred_element_type=F32)
    k_ref[...] = kv[:, :d]
    v_ref[...] = kv[:, d:]
    lf = _log_sigmoid(jnp.dot(xb, fw_ref[...], preferred_element_type=F32) + fb_ref[...])
    lf_ref[...] = lf[:, :nh]


def _kv_tok(x, wkv, fw, fb, *, nh):
    n, d = x.shape
    args = (x, wkv, fw, fb)
    return pl.pallas_call(
        _kv_tok_kernel,
        grid=(1,),
        in_specs=[_const_spec(a.shape) for a in args],
        out_specs=[_const_spec((n, d)), _const_spec((n, d)), _const_spec((n, nh))],
        out_shape=[jax.ShapeDtypeStruct((n, d), F32), jax.ShapeDtypeStruct((n, d), F32),
                   jax.ShapeDtypeStruct((n, nh), F32)],
        compiler_params=_cparams(1),
        name="kv_tok",
    )(*args)


def _proj_kernel(x_ref, w_ref, o_ref, *, scale):
    y = jnp.dot(x_ref[...].astype(BF16), w_ref[...], preferred_element_type=F32)
    o_ref[...] = (y * scale).astype(o_ref.dtype)


def _proj(x, w, *, scale, out_dtype, tm):
    m, d = x.shape
    n = w.shape[1]
    return pl.pallas_call(
        functools.partial(_proj_kernel, scale=scale),
        grid=(m // tm,),
        in_specs=[pl.BlockSpec((tm, d), lambda i: (i, 0)), _resident_spec(w)],
        out_specs=pl.BlockSpec((tm, n), lambda i: (i, 0)),
        out_shape=jax.ShapeDtypeStruct((m, n), out_dtype),
        compiler_params=_cparams(1),
        name="proj",
    )(x, _operand(w))


def _proj_ln_kernel(a_ref, x_ref, w_ref, g_ref, b_ref, o_ref, *, alpha):
    y = jnp.dot(a_ref[...].astype(BF16), w_ref[...], preferred_element_type=F32)
    o_ref[...] = _layer_norm(alpha * x_ref[...] + y, g_ref[...], b_ref[...])


def _proj_ln(a, x, w, g, b, *, alpha, tm):
    m, d = x.shape
    row = lambda i: (i, 0)
    return pl.pallas_call(
        functools.partial(_proj_ln_kernel, alpha=alpha),
        grid=(m // tm,),
        in_specs=[pl.BlockSpec((tm, a.shape[1]), row), pl.BlockSpec((tm, d), row),
                  _resident_spec(w), _const_spec(g.shape), _const_spec(b.shape)],
        out_specs=pl.BlockSpec((tm, d), row),
        out_shape=jax.ShapeDtypeStruct((m, d), F32),
        compiler_params=_cparams(1),
        name="proj_ln",
    )(a, x, _operand(w), g, b)


def _attn_ffn_kernel(a_ref, x_ref, w_ref, g_ref, b_ref, wgu_ref, wd_ref, g1_ref, b1_ref, *refs, alpha, q_scale):
    y = jnp.dot(a_ref[0], w_ref[...], preferred_element_type=F32)
    x1 = _layer_norm(alpha * x_ref[0] + y, g_ref[...], b_ref[...])
    x2 = _ffn_value(x1, wgu_ref, wd_ref, g1_ref, b1_ref, alpha)
    if len(refs) == 1:
        (o_ref,) = refs
    else:
        wqt_ref, o_ref, qt_ref = refs
        qt_ref[0] = _q_transposed(x2, wqt_ref, q_scale)
    o_ref[0] = x2


def _attn_ffn(a, x, w, g, b, wgu, wd, g1, b1, wqt, *, alpha, tm, q_scale):
    bsz, t, d = x.shape
    tok = pl.BlockSpec((1, tm, d), lambda i, j: (i, j, 0))
    consts = (w, g, b, wgu, wd, g1, b1) + (() if wqt is None else (wqt,))
    out_specs, out_shape = [tok], [jax.ShapeDtypeStruct((bsz, t, d), F32)]
    if wqt is not None:
        out_specs.append(pl.BlockSpec((1, wqt.shape[0], tm), lambda i, j: (i, 0, j)))
        out_shape.append(jax.ShapeDtypeStruct((bsz, wqt.shape[0], t), BF16))
    out = pl.pallas_call(
        functools.partial(_attn_ffn_kernel, alpha=alpha, q_scale=q_scale),
        grid=(bsz, t // tm),
        in_specs=[tok, tok] + [_resident_spec(c) for c in consts],
        out_specs=out_specs,
        out_shape=out_shape,
        compiler_params=_cparams(2),
        name="attn_ffn",
    )(a, x, *map(_operand, consts))
    return out[0] if wqt is None else out


def _fa_kernel(qt_ref, k_ref, ak_ref, vt_ref, c_ref, o_ref, m_ref, acc_ref, st_a_ref, st_b_ref, *, tq, dh):
    i = pl.program_id(1)
    hpt = LANES // dh
    vrows = dh + V_PAD
    n_groups = k_ref.shape[2] // (LANES * FA_TILES)
    sub = lax.broadcasted_iota(jnp.int32, (LANES, tq), 0)
    key_idx = lax.broadcasted_iota(jnp.int32, (tq, tq), 0)
    qry_idx = lax.broadcasted_iota(jnp.int32, (tq, tq), 1)

    def group(gi, carry):
        tiles = [gi * FA_TILES + lt for lt in range(FA_TILES)]
        los = [pl.multiple_of(tile * LANES, LANES) for tile in tiles]
        qp = []
        for tile, lo in zip(tiles, los):
            q2 = qt_ref[0, pl.ds(lo, LANES), :].astype(F32)
            c2 = c_ref[0, tile]
            for jh in range(hpt):
                qm = jnp.where((sub >= jh * dh) & (sub < (jh + 1) * dh), q2, 0.0)
                hi, mid, lw = _split3(c2[jh:jh + 1, :])
                base = AUG_PER_HEAD * jh
                aug = jnp.where(sub == base, hi, jnp.where(sub == base + 1, mid, jnp.where(sub == base + 2, lw,
                                jnp.where((sub >= base + N_SPLIT) & (sub < base + AUG_PER_HEAD), 1.0, 0.0))))
                qp.append(jnp.concatenate([qm, aug], axis=0).astype(BF16))
        m_ref[...] = jnp.full_like(m_ref, -jnp.inf)
        acc_ref[...] = jnp.zeros_like(acc_ref)

        def scores(kb, st_ref):
            ks = pl.multiple_of(kb * tq, tq)
            for lt, lo in enumerate(los):
                kk = jnp.concatenate([k_ref[0, pl.ds(ks, tq), pl.ds(lo, LANES)],
                                      ak_ref[0, pl.ds(ks, tq), pl.ds(lo, LANES)]], axis=1)
                for jh in range(hpt):
                    st_ref[lt * hpt + jh] = jnp.dot(kk, qp[lt * hpt + jh], preferred_element_type=F32)

        def accumulate(kb, st_ref, masked):
            ks = pl.multiple_of(kb * tq, tq)
            for lt, tile in enumerate(tiles):
                v2 = vt_ref[0, pl.ds(pl.multiple_of(tile * (hpt * vrows), hpt * vrows), hpt * vrows), pl.ds(ks, tq)]
                for jh in range(hpt):
                    hh = lt * hpt + jh
                    st = st_ref[hh]
                    if masked:
                        st = jnp.where(key_idx <= qry_idx, st, -jnp.inf)
                    m_old = m_ref[hh]
                    m_new = jnp.maximum(m_old, jnp.max(st, axis=0, keepdims=True))
                    p = jnp.exp2(st - m_new).astype(BF16)
                    pv = jnp.dot(v2[jh * vrows:(jh + 1) * vrows, :], p, preferred_element_type=F32)
                    acc_ref[hh] = jnp.exp2(m_old - m_new) * acc_ref[hh] + pv
                    m_ref[hh] = m_new

        def two_steps(tt, c):
            kb = 2 * tt
            scores(kb + 1, st_b_ref)
            accumulate(kb, st_a_ref, False)
            scores(kb + 2, st_a_ref)
            accumulate(kb + 1, st_b_ref, False)
            return c

        scores(0, st_a_ref)
        lax.fori_loop(0, i // 2, two_steps, 0)

        @pl.when(i % 2 == 1)
        def _():
            scores(i, st_b_ref)
            accumulate(i - 1, st_a_ref, False)
            accumulate(i, st_b_ref, True)

        @pl.when(i % 2 == 0)
        def _():
            accumulate(i, st_a_ref, True)

        for lt, lo in enumerate(los):
            heads = [acc_ref[lt * hpt + jh] for jh in range(hpt)]
            ot = jnp.concatenate([acc[:dh] / acc[dh:dh + 1] for acc in heads], axis=0)
            o_ref[0, :, pl.ds(lo, LANES)] = ot.T.astype(o_ref.dtype)
        return carry

    lax.fori_loop(0, n_groups, group, 0)


def _flash_attn(qt, kb, ak, vtb, ct, *, nh, tq):
    bsz, d, t = qt.shape
    dh = d // nh
    hpt = LANES // dh
    n_tiles = nh // hpt
    heads_per_step = FA_TILES * hpt
    assert n_tiles % FA_TILES == 0
    res3 = lambda shape: pl.BlockSpec(shape, lambda b, i: (b, 0, 0), pipeline_mode=pl.Buffered(1))
    return pl.pallas_call(
        functools.partial(_fa_kernel, tq=tq, dh=dh),
        grid=(bsz, t // tq),
        in_specs=[
            pl.BlockSpec((1, d, tq), lambda b, i: (b, 0, i)),
            res3((1, t, d)), res3((1, t, d)), res3((1, vtb.shape[1], t)),
            pl.BlockSpec((1, n_tiles, hpt, tq), lambda b, i: (b, 0, 0, i)),
        ],
        out_specs=pl.BlockSpec((1, tq, d), lambda b, i: (b, i, 0)),
        out_shape=jax.ShapeDtypeStruct((bsz, t, d), BF16),
        scratch_shapes=[pltpu.VMEM((heads_per_step, 1, tq), F32),
                        pltpu.VMEM((heads_per_step, dh + V_PAD, tq), F32),
                        pltpu.VMEM((heads_per_step, tq, tq), F32), pltpu.VMEM((heads_per_step, tq, tq), F32)],
        compiler_params=_cparams(2),
        name="flash_attn",
    )(qt, kb, ak, vtb, ct.reshape(bsz, n_tiles, hpt, t))


def _decode_kernel(pt_ref, q_ref, kn_ref, vn_ref, lfn_ref, *refs, pages, nh, dh):
    del pt_ref
    kt_refs = refs[:pages]
    vt_refs = refs[pages:2 * pages]
    lf_refs = refs[2 * pages:3 * pages]
    o_ref = refs[3 * pages]
    qbd_ref, m_ref, l_ref, acc_ref, r_ref = refs[3 * pages + 1:]
    c = pl.program_id(1)
    d = nh * dh
    page = kt_refs[0].shape[2]
    head_row = lax.broadcasted_iota(jnp.int32, (nh, d), 0)
    head_of_lane = lax.broadcasted_iota(jnp.int32, (nh, d), 1) // dh
    block_diag = head_of_lane == head_row

    @pl.when(c == 0)
    def _():
        qbd = jnp.where(block_diag, jnp.broadcast_to(q_ref[0], (nh, d)), 0.0)
        qbd_ref[...] = qbd.astype(BF16)
        m_ref[...] = jnp.sum(qbd * kn_ref[0], axis=-1, keepdims=True)
        l_ref[...] = jnp.ones_like(l_ref)
        acc_ref[...] = jnp.broadcast_to(vn_ref[0], (nh, d))
        eye = (lax.broadcasted_iota(jnp.int32, (nh, nh), 0)
               == lax.broadcasted_iota(jnp.int32, (nh, nh), 1))
        lfn = jnp.broadcast_to(lfn_ref[0], (nh, nh))
        r_ref[...] = jnp.sum(jnp.where(eye, lfn, 0.0), axis=-1, keepdims=True)

    qbd = qbd_ref[...]
    r = r_ref[...]
    key_lane = lax.broadcasted_iota(jnp.int32, (nh, page), 1)
    scores = [None] * pages
    for pg in reversed(range(pages)):
        s = jnp.dot(qbd, kt_refs[pg][0].astype(BF16), preferred_element_type=F32)
        lft = lf_refs[pg][0]
        inc = lft
        shift = 1
        while shift < page:
            moved = pltpu.roll(inc, page - shift, 1)
            inc = inc + jnp.where(key_lane < page - shift, moved, 0.0)
            shift *= 2
        scores[pg] = s + ((inc - lft) + r)
        r = r + inc[:, 0:1]
    r_ref[...] = r
    s_all = jnp.concatenate(scores, axis=1)
    m_old = m_ref[...]
    m_new = jnp.maximum(m_old, jnp.max(s_all, axis=-1, keepdims=True))
    a = jnp.exp(m_old - m_new)
    p = jnp.exp(s_all - m_new)
    l_ref[...] = a * l_ref[...] + jnp.sum(p, axis=-1, keepdims=True)
    p = p.astype(BF16)
    pv = None
    for pg in range(pages):
        part = lax.dot_general(p[:, pg * page:(pg + 1) * page], vt_refs[pg][0].astype(BF16), _NT,
                               preferred_element_type=F32)
        pv = part if pv is None else pv + part
    acc_ref[...] = a * acc_ref[...] + pv
    m_ref[...] = m_new

    @pl.when(c == pl.num_programs(1) - 1)
    def _():
        o_ref[0] = jnp.sum(jnp.where(block_diag, acc_ref[...] / l_ref[...], 0.0), axis=0, keepdims=True)


def _decode_attn(page_table, q, k_new, v_new, lf_new, cache_kt, cache_vt, cache_lft, *, nh, pages):
    n, d = q.shape
    dh = d // nh
    page = cache_lft.shape[2]
    n_chunks = page_table.shape[1] // pages

    def tok_spec(width):
        return pl.BlockSpec((1, 1, width), lambda b, c, pt: (b, 0, 0))

    def page_spec(rows, pg):
        return pl.BlockSpec(
            (1, rows, page),
            lambda b, c, pt: (pt[b, (n_chunks - 1 - c) * pages + pg], 0, 0))

    in_specs = [tok_spec(d), tok_spec(d), tok_spec(d), tok_spec(nh)]
    in_specs += [page_spec(d, pg) for pg in range(pages)]
    in_specs += [page_spec(d, pg) for pg in range(pages)]
    in_specs += [page_spec(nh, pg) for pg in range(pages)]
    grid_spec = pltpu.PrefetchScalarGridSpec(
        num_scalar_prefetch=1,
        grid=(n, n_chunks),
        in_specs=in_specs,
        out_specs=tok_spec(d),
        scratch_shapes=[
            pltpu.VMEM((nh, d), BF16), pltpu.VMEM((nh, 1), F32), pltpu.VMEM((nh, 1), F32),
            pltpu.VMEM((nh, d), F32), pltpu.VMEM((nh, 1), F32),
        ],
    )
    out = pl.pallas_call(
        functools.partial(_decode_kernel, pages=pages, nh=nh, dh=dh),
        grid_spec=grid_spec,
        out_shape=jax.ShapeDtypeStruct((n, 1, d), F32),
        compiler_params=_cparams(2),
        name="decode_attn",
    )(page_table, q.reshape(n, 1, d), k_new.reshape(n, 1, d), v_new.reshape(n, 1, d),
      lf_new.reshape(n, 1, nh), *([cache_kt] * pages), *([cache_vt] * pages), *([cache_lft] * pages))
    return out.reshape(n, d)


def _bias_placement(nh, dh, d):
    hpt = LANES // dh
    place = np.zeros((LANES, d), np.float32)
    for h in range(nh):
        col0 = (h // hpt) * LANES + AUG_PER_HEAD * (h % hpt)
        for piece in range(N_SPLIT):
            place[N_SPLIT * nh, col0 + piece] = 1.0
            place[piece * nh + h, col0 + N_SPLIT + piece] = -1.0
    return place


def kernel(x_prompt, x_sample, cache_k, cache_v, cache_logf, state_conv, page_table, a_w_in, a_conv_w, a_w_out, b_w_q, b_w_o, kv_w, f_w, f_b, ffn_w_gu, ffn_w_down, ln_g, ln_b):
    bsz, t, d = x_prompt.shape
    n_s = x_sample.shape[0]
    n_a, n_b = a_w_in.shape[0], b_w_q.shape[0]
    depth = n_a + n_b
    nh = f_w.shape[1]
    dh = d // nh
    n_pool, page = cache_logf.shape[0], cache_logf.shape[1]
    alpha = (2 * depth) ** 0.25
    attn_scale = dh ** -0.5
    q_scale = attn_scale * LOG2E
    tm_seq, tm_fused, tq, dec_pages = 512, 256, 512, 16
    assert LANES % dh == 0 and (N_SPLIT + 1) * nh <= LANES and AUG_PER_HEAD * (LANES // dh) <= LANES

    def layers(stack):
        return [_Layer(stack, l) for l in range(stack.shape[0])]

    w_in, w_out = layers(a_w_in.astype(BF16)), layers(a_w_out.astype(BF16))
    w_q, w_o = layers(b_w_q.astype(BF16)), layers(b_w_o.astype(BF16))
    w_qt = layers(jnp.swapaxes(b_w_q, 1, 2).astype(BF16))
    w_kv = kv_w.astype(BF16)
    w_kvt = kv_w.T.astype(BF16)
    w_gu, w_down = layers(ffn_w_gu.astype(BF16)), layers(ffn_w_down.astype(BF16))
    fw_pad = jnp.pad(f_w, ((0, 0), (0, LANES - nh))).astype(BF16)
    fb_pad = jnp.pad(f_b, (0, LANES - nh)).reshape(1, LANES)
    fw_rep = jnp.pad(jnp.tile(f_w, (1, N_SPLIT)), ((0, 0), (0, LANES - N_SPLIT * nh))).astype(BF16)
    fb_rep = jnp.pad(jnp.tile(f_b, N_SPLIT), (0, LANES - N_SPLIT * nh)).reshape(1, LANES)
    fwt = f_w.T.astype(BF16)
    fbt = f_b.reshape(nh, 1)
    place = jnp.asarray(_bias_placement(nh, dh, d), BF16)
    g = ln_g.reshape(depth, 2, 1, d)
    b = ln_b.reshape(depth, 2, 1, d)

    x = x_prompt
    hist0 = jnp.zeros((bsz, 2, d), F32)
    conv_p = []
    for l in range(n_a):
        x, st = _conv_seq(x, hist0, w_in[l], a_conv_w[l], w_out[l], g[l, 0], b[l, 0],
                          w_gu[l], w_down[l], g[l, 1], b[l, 1], alpha=alpha, tm=tm_fused)
        conv_p.append(st)
    kt_p, vt_p, lft_p, kb, ak, vtb, ct, qt = _kv_seq(x, w_kvt, fw_rep, fb_rep, fwt, fbt, place,
                                                     w_qt[0], nh=nh, tm=tm_seq, q_scale=q_scale)
    for j in range(n_b):
        l = n_a + j
        o = _flash_attn(qt, kb, ak, vtb, ct, nh=nh, tq=tq)
        w_next = w_qt[j + 1] if j + 1 < n_b else None
        out = _attn_ffn(o, x, w_o[j], g[l, 0], b[l, 0], w_gu[l], w_down[l], g[l, 1], b[l, 1], w_next,
                        alpha=alpha, tm=tm_fused, q_scale=q_scale)
        x, qt = out if w_next is not None else (out, None)
    y_prompt = x

    cache_kt = jnp.transpose(cache_k, (0, 2, 3, 1)).reshape(n_pool, d, page)
    cache_vt = jnp.transpose(cache_v, (0, 2, 3, 1)).reshape(n_pool, d, page)
    cache_lft = jnp.transpose(cache_logf, (0, 2, 1))
    xs = x_sample.reshape(n_s, d)
    conv_s = []
    for l in range(n_a):
        xs, st = _conv_tok(xs, state_conv[l].reshape(n_s, 2 * d), w_in[l], a_conv_w[l], w_out[l],
                           g[l, 0], b[l, 0], alpha=alpha)
        conv_s.append(st.reshape(n_s, 2, d))
        xs = _ffn(xs, w_gu[l], w_down[l], g[l, 1], b[l, 1], alpha=alpha, tm=n_s)
    k_s, v_s, lf_s = _kv_tok(xs, w_kv, fw_pad, fb_pad, nh=nh)
    for j in range(n_b):
        l = n_a + j
        qs = _proj(xs, w_q[j], scale=attn_scale, out_dtype=F32, tm=n_s)
        os_ = _decode_attn(page_table, qs, k_s, v_s, lf_s, cache_kt, cache_vt, cache_lft, nh=nh, pages=dec_pages)
        xs = _proj_ln(os_, xs, w_o[j], g[l, 0], b[l, 0], alpha=alpha, tm=n_s)
        xs = _ffn(xs, w_gu[l], w_down[l], g[l, 1], b[l, 1], alpha=alpha, tm=n_s)

    return (y_prompt, xs.reshape(n_s, 1, d),
            jnp.transpose(kt_p, (0, 3, 1, 2)), jnp.transpose(vt_p, (0, 3, 1, 2)), jnp.transpose(lft_p, (0, 2, 1)),
            jnp.stack(conv_p),
            k_s.reshape(n_s, 1, nh, dh), v_s.reshape(n_s, 1, nh, dh), lf_s.reshape(n_s, 1, nh),
            jnp.stack(conv_s))
```

```python
import functools
import math

import numpy as np
import jax
import jax.numpy as jnp
from jax import lax
from jax.experimental import pallas as pl
from jax.experimental.pallas import tpu as pltpu

F32 = jnp.float32
BF16 = jnp.bfloat16
LN_EPS = 1e-5
LOG2E = math.log2(math.e)
V7X_VMEM_LIMIT_BYTES = 56 * 1024 * 1024
LANES = 128
N_SPLIT = 3
AUG_PER_HEAD = 2 * N_SPLIT
V_PAD = 16
FA_TILES = 4

_NT = (((1,), (1,)), ((), ()))


def _cparams(n_axes):
    return pltpu.CompilerParams(
        dimension_semantics=("arbitrary",) * n_axes,
        vmem_limit_bytes=V7X_VMEM_LIMIT_BYTES,
    )


def _const_spec(shape):
    nd = len(shape)
    return pl.BlockSpec(shape, lambda *_: (0,) * nd, pipeline_mode=pl.Buffered(1))


class _Layer:
    def __init__(self, stack, index):
        self.stack, self.index = stack, index

    @property
    def shape(self):
        return self.stack.shape[1:]


def _resident_spec(c):
    if not isinstance(c, _Layer):
        return _const_spec(c.shape)
    nd, layer = c.stack.ndim, c.index
    return pl.BlockSpec((None,) + c.shape, lambda *_: (layer,) + (0,) * (nd - 1), pipeline_mode=pl.Buffered(1))


def _operand(c):
    return c.stack if isinstance(c, _Layer) else c


def _layer_norm(r, g, b):
    mu = jnp.mean(r, axis=-1, keepdims=True)
    c = r - mu
    var = jnp.mean(c * c, axis=-1, keepdims=True)
    return c * lax.rsqrt(var + LN_EPS) * g + b


def _log_sigmoid(z):
    return jnp.minimum(z, 0.0) - jnp.log1p(jnp.exp(-jnp.abs(z)))


def _split3(c):
    hi = c.astype(BF16).astype(F32)
    r = c - hi
    mid = r.astype(BF16).astype(F32)
    lo = (r - mid).astype(BF16).astype(F32)
    return hi, mid, lo


def _gated_conv_out(x, bch, p1, p2, cw_ref, w_out_ref, g_ref, b_ref, alpha):
    d = x.shape[-1]
    u = bch[:, d:2 * d] * bch[:, 2 * d:]
    z = cw_ref[2:3, :] * u
    z = z + cw_ref[0:1, :] * p2
    z = z + cw_ref[1:2, :] * p1
    y = jnp.dot((bch[:, :d] * z).astype(BF16), w_out_ref[...], preferred_element_type=F32)
    return _layer_norm(alpha * x + y, g_ref[...], b_ref[...])


def _ffn_value(x, wgu_ref, wd_ref, g_ref, b_ref, alpha):
    dff = wd_ref.shape[0]
    gu = jnp.dot(x.astype(BF16), wgu_ref[...], preferred_element_type=F32)
    gate = gu[:, :dff]
    a = (gate * jax.nn.sigmoid(gate)) * gu[:, dff:]
    y = jnp.dot(a.astype(BF16), wd_ref[...], preferred_element_type=F32)
    return _layer_norm(alpha * x + y, g_ref[...], b_ref[...])


def _q_transposed(x, wqt_ref, scale):
    y = lax.dot_general(wqt_ref[...], x.astype(BF16), _NT, preferred_element_type=F32)
    return (y * scale).astype(BF16)


def _conv_seq_kernel(x_ref, hist_ref, w_in_ref, cw_ref, w_out_ref, g_ref, b_ref,
                     o_ref, st_ref, carry_ref, *, alpha):
    j = pl.program_id(1)

    @pl.when(j == 0)
    def _():
        carry_ref[...] = hist_ref[0]

    x = x_ref[0]
    tm, d = x.shape
    bch = jnp.dot(x.astype(BF16), w_in_ref[...], preferred_element_type=F32)
    u = bch[:, d:2 * d] * bch[:, 2 * d:]
    h0 = carry_ref[0:1, :]
    h1 = carry_ref[1:2, :]
    row = lax.broadcasted_iota(jnp.int32, (tm, d), 0)
    p1 = jnp.where(row == 0, h1, pltpu.roll(u, 1, 0))
    p2 = jnp.where(row == 0, h0, jnp.where(row == 1, h1, pltpu.roll(u, 2, 0)))
    tail = u[tm - 2:, :]
    carry_ref[...] = tail
    o_ref[0] = _gated_conv_out(x, bch, p1, p2, cw_ref, w_out_ref, g_ref, b_ref, alpha)

    @pl.when(j == pl.num_programs(1) - 1)
    def _():
        st_ref[0] = tail


def _conv_seq(x, hist, w_in, cw, w_out, g, b, *, alpha, tm):
    bsz, t, d = x.shape
    consts = (w_in, cw, w_out, g, b)
    return pl.pallas_call(
        functools.partial(_conv_seq_kernel, alpha=alpha),
        grid=(bsz, t // tm),
        in_specs=[
            pl.BlockSpec((1, tm, d), lambda i, j: (i, j, 0)),
            pl.BlockSpec((1, 2, d), lambda i, j: (i, 0, 0)),
        ] + [_resident_spec(c) for c in consts],
        out_specs=[
            pl.BlockSpec((1, tm, d), lambda i, j: (i, j, 0)),
            pl.BlockSpec((1, 2, d), lambda i, j: (i, 0, 0)),
        ],
        out_shape=[jax.ShapeDtypeStruct((bsz, t, d), F32), jax.ShapeDtypeStruct((bsz, 2, d), F32)],
        scratch_shapes=[pltpu.VMEM((2, d), F32)],
        compiler_params=_cparams(2),
        name="conv_seq",
    )(x, hist, *map(_operand, consts))


def _conv_tok_kernel(x_ref, hist_ref, w_in_ref, cw_ref, w_out_ref, g_ref, b_ref,
                     o_ref, st_ref, *, alpha):
    x = x_ref[...]
    d = x.shape[-1]
    bch = jnp.dot(x.astype(BF16), w_in_ref[...], preferred_element_type=F32)
    u = bch[:, d:2 * d] * bch[:, 2 * d:]
    h0 = hist_ref[:, :d]
    h1 = hist_ref[:, d:]
    o_ref[...] = _gated_conv_out(x, bch, h1, h0, cw_ref, w_out_ref, g_ref, b_ref, alpha)
    st_ref[:, :d] = h1
    st_ref[:, d:] = u


def _conv_tok(x, hist, w_in, cw, w_out, g, b, *, alpha):
    n, d = x.shape
    args = (x, hist, w_in, cw, w_out, g, b)
    return pl.pallas_call(
        functools.partial(_conv_tok_kernel, alpha=alpha),
        grid=(1,),
        in_specs=[_resident_spec(a) for a in args],
        out_specs=[_const_spec((n, d)), _const_spec((n, 2 * d))],
        out_shape=[jax.ShapeDtypeStruct((n, d), F32), jax.ShapeDtypeStruct((n, 2 * d), F32)],
        compiler_params=_cparams(1),
        name="conv_tok",
    )(*map(_operand, args))


def _ffn_kernel(x_ref, wgu_ref, wd_ref, g_ref, b_ref, o_ref, *, alpha):
    o_ref[...] = _ffn_value(x_ref[...], wgu_ref, wd_ref, g_ref, b_ref, alpha)


def _ffn(x, wgu, wd, g, b, *, alpha, tm):
    m, d = x.shape
    return pl.pallas_call(
        functools.partial(_ffn_kernel, alpha=alpha),
        grid=(m // tm,),
        in_specs=[
            pl.BlockSpec((tm, d), lambda i: (i, 0)),
            _resident_spec(wgu), _resident_spec(wd), _const_spec(g.shape), _const_spec(b.shape),
        ],
        out_specs=pl.BlockSpec((tm, d), lambda i: (i, 0)),
        out_shape=jax.ShapeDtypeStruct((m, d), F32),
        compiler_params=_cparams(1),
        name="ffn",
    )(x, _operand(wgu), _operand(wd), g, b)


def _kv_seq_kernel(x_ref, wkvt_ref, fw_ref, fb_ref, fwt_ref, fbt_ref, place_ref, wqt_ref,
                   kt_ref, vt_ref, lft_ref, kb_ref, ak_ref, vtb_ref, ct_ref, qt_ref,
                   carry_ref, carry_t_ref, *, q_scale):
    j = pl.program_id(1)

    @pl.when(j == 0)
    def _():
        carry_ref[...] = jnp.zeros_like(carry_ref)
        carry_t_ref[...] = jnp.zeros_like(carry_t_ref)

    xb = x_ref[0].astype(BF16)
    tm, d = xb.shape
    nh = lft_ref.shape[1]
    kvt = lax.dot_general(wkvt_ref[...], xb, _NT, preferred_element_type=F32)
    kt_ref[0] = kvt[:d].reshape(kt_ref.shape[1:])
    vt3 = kvt[d:].reshape(vt_ref.shape[1:])
    vt_ref[0] = vt3
    ones = jnp.ones((nh, V_PAD, tm), F32)
    vtb_ref[0] = jnp.concatenate([vt3, ones], axis=1).reshape(vtb_ref.shape[1:]).astype(BF16)
    kb_ref[0] = kvt[:d].T.astype(BF16)
    qt_ref[0] = _q_transposed(xb, wqt_ref, q_scale)

    lft = _log_sigmoid(lax.dot_general(fwt_ref[...], xb, _NT, preferred_element_type=F32) + fbt_ref[...])
    lft_ref[0] = lft
    lane = lax.broadcasted_iota(jnp.int32, lft.shape, 1)
    ct = lft
    shift = 1
    while shift < tm:
        ct = ct + jnp.where(lane >= shift, pltpu.roll(ct, shift, 1), 0.0)
        shift *= 2
    ct = ct + carry_t_ref[...]
    carry_t_ref[...] = ct[:, tm - 1:]
    ct_ref[0] = ct * LOG2E

    lf = _log_sigmoid(jnp.dot(xb, fw_ref[...], preferred_element_type=F32) + fb_ref[...])
    row = lax.broadcasted_iota(jnp.int32, lf.shape, 0)
    c = lf
    shift = 1
    while shift < tm:
        c = c + jnp.where(row >= shift, pltpu.roll(c, shift, 0), 0.0)
        shift *= 2
    c = c + carry_ref[...]
    carry_ref[...] = c[tm - 1:, :]
    hi, mid, lo = _split3(c * LOG2E)
    grp = lax.broadcasted_iota(jnp.int32, lf.shape, 1) // nh
    pieces = jnp.where(grp == 0, hi, jnp.where(grp == 1, mid, jnp.where(grp == 2, lo,
                       jnp.where(grp == N_SPLIT, 1.0, 0.0))))
    ak_ref[0] = jnp.dot(pieces.astype(BF16), place_ref[...], preferred_element_type=F32).astype(BF16)


def _kv_seq(x, wkvt, fw, fb, fwt, fbt, place, wqt, *, nh, tm, q_scale):
    bsz, t, d = x.shape
    dh = d // nh
    vrows = nh * (dh + V_PAD)
    tok = pl.BlockSpec((1, tm, d), lambda i, j: (i, j, 0))
    tr4 = pl.BlockSpec((1, nh, dh, tm), lambda i, j: (i, 0, 0, j))
    tr3 = pl.BlockSpec((1, nh, tm), lambda i, j: (i, 0, j))
    trv = pl.BlockSpec((1, vrows, tm), lambda i, j: (i, 0, j))
    trq = pl.BlockSpec((1, wqt.shape[0], tm), lambda i, j: (i, 0, j))
    consts = (wkvt, fw, fb, fwt, fbt, place, wqt)
    return pl.pallas_call(
        functools.partial(_kv_seq_kernel, q_scale=q_scale),
        grid=(bsz, t // tm),
        in_specs=[tok] + [_resident_spec(a) for a in consts],
        out_specs=[tr4, tr4, tr3, tok, tok, trv, tr3, trq],
        out_shape=[
            jax.ShapeDtypeStruct((bsz, nh, dh, t), F32), jax.ShapeDtypeStruct((bsz, nh, dh, t), F32),
            jax.ShapeDtypeStruct((bsz, nh, t), F32),
            jax.ShapeDtypeStruct((bsz, t, d), BF16), jax.ShapeDtypeStruct((bsz, t, d), BF16),
            jax.ShapeDtypeStruct((bsz, vrows, t), BF16), jax.ShapeDtypeStruct((bsz, nh, t), F32),
            jax.ShapeDtypeStruct((bsz, wqt.shape[0], t), BF16),
        ],
        scratch_shapes=[pltpu.VMEM((1, fw.shape[1]), F32), pltpu.VMEM((nh, 1), F32)],
        compiler_params=_cparams(2),
        name="kv_seq",
    )(x, *map(_operand, consts))


def _kv_tok_kernel(x_ref, wkv_ref, fw_ref, fb_ref, k_ref, v_ref, lf_ref):
    xb = x_ref[...].astype(BF16)
    d = xb.shape[1]
    nh = lf_ref.shape[-1]
    kv = jnp.dot(xb, wkv_ref[...], preferred_element_type=F32)
    k_ref[...] = kv[:, :d]
    v_ref[...] = kv[:, d:]
    lf = _log_sigmoid(jnp.dot(xb, fw_ref[...], preferred_element_type=F32) + fb_ref[...])
    lf_ref[...] = lf[:, :nh]


def _kv_tok(x, wkv, fw, fb, *, nh):
    n, d = x.shape
    args = (x, wkv, fw, fb)
    return pl.pallas_call(
        _kv_tok_kernel,
        grid=(1,),
        in_specs=[_const_spec(a.shape) for a in args],
        out_specs=[_const_spec((n, d)), _const_spec((n, d)), _const_spec((n, nh))],
        out_shape=[jax.ShapeDtypeStruct((n, d), F32), jax.ShapeDtypeStruct((n, d), F32),
                   jax.ShapeDtypeStruct((n, nh), F32)],
        compiler_params=_cparams(1),
        name="kv_tok",
    )(*args)


def _proj_kernel(x_ref, w_ref, o_ref, *, scale):
    y = jnp.dot(x_ref[...].astype(BF16), w_ref[...], preferred_element_type=F32)
    o_ref[...] = (y * scale).astype(o_ref.dtype)


def _proj(x, w, *, scale, out_dtype, tm):
    m, d = x.shape
    n = w.shape[1]
    return pl.pallas_call(
        functools.partial(_proj_kernel, scale=scale),
        grid=(m // tm,),
        in_specs=[pl.BlockSpec((tm, d), lambda i: (i, 0)), _resident_spec(w)],
        out_specs=pl.BlockSpec((tm, n), lambda i: (i, 0)),
        out_shape=jax.ShapeDtypeStruct((m, n), out_dtype),
        compiler_params=_cparams(1),
        name="proj",
    )(x, _operand(w))


def _proj_ln_kernel(a_ref, x_ref, w_ref, g_ref, b_ref, o_ref, *, alpha):
    y = jnp.dot(a_ref[...].astype(BF16), w_ref[...], preferred_element_type=F32)
    o_ref[...] = _layer_norm(alpha * x_ref[...] + y, g_ref[...], b_ref[...])


def _proj_ln(a, x, w, g, b, *, alpha, tm):
    m, d = x.shape
    row = lambda i: (i, 0)
    return pl.pallas_call(
        functools.partial(_proj_ln_kernel, alpha=alpha),
        grid=(m // tm,),
        in_specs=[pl.BlockSpec((tm, a.shape[1]), row), pl.BlockSpec((tm, d), row),
                  _resident_spec(w), _const_spec(g.shape), _const_spec(b.shape)],
        out_specs=pl.BlockSpec((tm, d), row),
        out_shape=jax.ShapeDtypeStruct((m, d), F32),
        compiler_params=_cparams(1),
        name="proj_ln",
    )(a, x, _operand(w), g, b)


def _attn_ffn_kernel(a_ref, x_ref, w_ref, g_ref, b_ref, wgu_ref, wd_ref, g1_ref, b1_ref, *refs, alpha, q_scale):
    y = jnp.dot(a_ref[0], w_ref[...], preferred_element_type=F32)
    x1 = _layer_norm(alpha * x_ref[0] + y, g_ref[...], b_ref[...])
    x2 = _ffn_value(x1, wgu_ref, wd_ref, g1_ref, b1_ref, alpha)
    if len(refs) == 1:
        (o_ref,) = refs
    else:
        wqt_ref, o_ref, qt_ref = refs
        qt_ref[0] = _q_transposed(x2, wqt_ref, q_scale)
    o_ref[0] = x2


def _attn_ffn(a, x, w, g, b, wgu, wd, g1, b1, wqt, *, alpha, tm, q_scale):
    bsz, t, d = x.shape
    tok = pl.BlockSpec((1, tm, d), lambda i, j: (i, j, 0))
    consts = (w, g, b, wgu, wd, g1, b1) + (() if wqt is None else (wqt,))
    out_specs, out_shape = [tok], [jax.ShapeDtypeStruct((bsz, t, d), F32)]
    if wqt is not None:
        out_specs.append(pl.BlockSpec((1, wqt.shape[0], tm), lambda i, j: (i, 0, j)))
        out_shape.append(jax.ShapeDtypeStruct((bsz, wqt.shape[0], t), BF16))
    out = pl.pallas_call(
        functools.partial(_attn_ffn_kernel, alpha=alpha, q_scale=q_scale),
        grid=(bsz, t // tm),
        in_specs=[tok, tok] + [_resident_spec(c) for c in consts],
        out_specs=out_specs,
        out_shape=out_shape,
        compiler_params=_cparams(2),
        name="attn_ffn",
    )(a, x, *map(_operand, consts))
    return out[0] if wqt is None else out


def _fa_kernel(qt_ref, k_ref, ak_ref, vt_ref, c_ref, o_ref, m_ref, acc_ref, st_a_ref, st_b_ref, *, tq, dh):
    i = pl.program_id(1)
    hpt = LANES // dh
    vrows = dh + V_PAD
    n_groups = k_ref.shape[2] // (LANES * FA_TILES)
    sub = lax.broadcasted_iota(jnp.int32, (LANES, tq), 0)
    key_idx = lax.broadcasted_iota(jnp.int32, (tq, tq), 0)
    qry_idx = lax.broadcasted_iota(jnp.int32, (tq, tq), 1)

    def group(gi, carry):
        tiles = [gi * FA_TILES + lt for lt in range(FA_TILES)]
        los = [pl.multiple_of(tile * LANES, LANES) for tile in tiles]
        qp = []
        for tile, lo in zip(tiles, los):
            q2 = qt_ref[0, pl.ds(lo, LANES), :].astype(F32)
            c2 = c_ref[0, tile]
            for jh in range(hpt):
                qm = jnp.where((sub >= jh * dh) & (sub < (jh + 1) * dh), q2, 0.0)
                hi, mid, lw = _split3(c2[jh:jh + 1, :])
                base = AUG_PER_HEAD * jh
                aug = jnp.where(sub == base, hi, jnp.where(sub == base + 1, mid, jnp.where(sub == base + 2, lw,
                                jnp.where((sub >= base + N_SPLIT) & (sub < base + AUG_PER_HEAD), 1.0, 0.0))))
                qp.append(jnp.concatenate([qm, aug], axis=0).astype(BF16))
        m_ref[...] = jnp.full_like(m_ref, -jnp.inf)
        acc_ref[...] = jnp.zeros_like(acc_ref)

        def scores(kb, st_ref):
            ks = pl.multiple_of(kb * tq, tq)
            for lt, lo in enumerate(los):
                kk = jnp.concatenate([k_ref[0, pl.ds(ks, tq), pl.ds(lo, LANES)],
                                      ak_ref[0, pl.ds(ks, tq), pl.ds(lo, LANES)]], axis=1)
                for jh in range(hpt):
                    st_ref[lt * hpt + jh] = jnp.dot(kk, qp[lt * hpt + jh], preferred_element_type=F32)

        def accumulate(kb, st_ref, masked):
            ks = pl.multiple_of(kb * tq, tq)
            for lt, tile in enumerate(tiles):
                v2 = vt_ref[0, pl.ds(pl.multiple_of(tile * (hpt * vrows), hpt * vrows), hpt * vrows), pl.ds(ks, tq)]
                for jh in range(hpt):
                    hh = lt * hpt + jh
                    st = st_ref[hh]
                    if masked:
                        st = jnp.where(key_idx <= qry_idx, st, -jnp.inf)
                    m_old = m_ref[hh]
                    m_new = jnp.maximum(m_old, jnp.max(st, axis=0, keepdims=True))
                    p = jnp.exp2(st - m_new).astype(BF16)
                    pv = jnp.dot(v2[jh * vrows:(jh + 1) * vrows, :], p, preferred_element_type=F32)
                    acc_ref[hh] = jnp.exp2(m_old - m_new) * acc_ref[hh] + pv
                    m_ref[hh] = m_new

        def two_steps(tt, c):
            kb = 2 * tt
            scores(kb + 1, st_b_ref)
            accumulate(kb, st_a_ref, False)
            scores(kb + 2, st_a_ref)
            accumulate(kb + 1, st_b_ref, False)
            return c

        scores(0, st_a_ref)
        lax.fori_loop(0, i // 2, two_steps, 0)

        @pl.when(i % 2 == 1)
        def _():
            scores(i, st_b_ref)
            accumulate(i - 1, st_a_ref, False)
            accumulate(i, st_b_ref, True)

        @pl.when(i % 2 == 0)
        def _():
            accumulate(i, st_a_ref, True)

        for lt, lo in enumerate(los):
            heads = [acc_ref[lt * hpt + jh] for jh in range(hpt)]
            ot = jnp.concatenate([acc[:dh] / acc[dh:dh + 1] for acc in heads], axis=0)
            o_ref[0, :, pl.ds(lo, LANES)] = ot.T.astype(o_ref.dtype)
        return carry

    lax.fori_loop(0, n_groups, group, 0)


def _flash_attn(qt, kb, ak, vtb, ct, *, nh, tq):
    bsz, d, t = qt.shape
    dh = d // nh
    hpt = LANES // dh
    n_tiles = nh // hpt
    heads_per_step = FA_TILES * hpt
    assert n_tiles % FA_TILES == 0
    res3 = lambda shape: pl.BlockSpec(shape, lambda b, i: (b, 0, 0), pipeline_mode=pl.Buffered(1))
    return pl.pallas_call(
        functools.partial(_fa_kernel, tq=tq, dh=dh),
        grid=(bsz, t // tq),
        in_specs=[
            pl.BlockSpec((1, d, tq), lambda b, i: (b, 0, i)),
            res3((1, t, d)), res3((1, t, d)), res3((1, vtb.shape[1], t)),
            pl.BlockSpec((1, n_tiles, hpt, tq), lambda b, i: (b, 0, 0, i)),
        ],
        out_specs=pl.BlockSpec((1, tq, d), lambda b, i: (b, i, 0)),
        out_shape=jax.ShapeDtypeStruct((bsz, t, d), BF16),
        scratch_shapes=[pltpu.VMEM((heads_per_step, 1, tq), F32),
                        pltpu.VMEM((heads_per_step, dh + V_PAD, tq), F32),
                        pltpu.VMEM((heads_per_step, tq, tq), F32), pltpu.VMEM((heads_per_step, tq, tq), F32)],
        compiler_params=_cparams(2),
        name="flash_attn",
    )(qt, kb, ak, vtb, ct.reshape(bsz, n_tiles, hpt, t))


def _decode_kernel(pt_ref, q_ref, kn_ref, vn_ref, lfn_ref, *refs, pages, nh, dh):
    del pt_ref
    kt_refs = refs[:pages]
    vt_refs = refs[pages:2 * pages]
    lf_refs = refs[2 * pages:3 * pages]
    o_ref = refs[3 * pages]
    qbd_ref, m_ref, l_ref, acc_ref, r_ref = refs[3 * pages + 1:]
    c = pl.program_id(1)
    d = nh * dh
    page = kt_refs[0].shape[2]
    head_row = lax.broadcasted_iota(jnp.int32, (nh, d), 0)
    head_of_lane = lax.broadcasted_iota(jnp.int32, (nh, d), 1) // dh
    block_diag = head_of_lane == head_row

    @pl.when(c == 0)
    def _():
        qbd = jnp.where(block_diag, jnp.broadcast_to(q_ref[0], (nh, d)), 0.0)
        qbd_ref[...] = qbd.astype(BF16)
        m_ref[...] = jnp.sum(qbd * kn_ref[0], axis=-1, keepdims=True)
        l_ref[...] = jnp.ones_like(l_ref)
        acc_ref[...] = jnp.broadcast_to(vn_ref[0], (nh, d))
        eye = (lax.broadcasted_iota(jnp.int32, (nh, nh), 0)
               == lax.broadcasted_iota(jnp.int32, (nh, nh), 1))
        lfn = jnp.broadcast_to(lfn_ref[0], (nh, nh))
        r_ref[...] = jnp.sum(jnp.where(eye, lfn, 0.0), axis=-1, keepdims=True)

    qbd = qbd_ref[...]
    r = r_ref[...]
    key_lane = lax.broadcasted_iota(jnp.int32, (nh, page), 1)
    scores = [None] * pages
    for pg in reversed(range(pages)):
        s = jnp.dot(qbd, kt_refs[pg][0].astype(BF16), preferred_element_type=F32)
        lft = lf_refs[pg][0]
        inc = lft
        shift = 1
        while shift < page:
            moved = pltpu.roll(inc, page - shift, 1)
            inc = inc + jnp.where(key_lane < page - shift, moved, 0.0)
            shift *= 2
        scores[pg] = s + ((inc - lft) + r)
        r = r + inc[:, 0:1]
    r_ref[...] = r
    s_all = jnp.concatenate(scores, axis=1)
    m_old = m_ref[...]
    m_new = jnp.maximum(m_old, jnp.max(s_all, axis=-1, keepdims=True))
    a = jnp.exp(m_old - m_new)
    p = jnp.exp(s_all - m_new)
    l_ref[...] = a * l_ref[...] + jnp.sum(p, axis=-1, keepdims=True)
    p = p.astype(BF16)
    pv = None
    for pg in range(pages):
        part = lax.dot_general(p[:, pg * page:(pg + 1) * page], vt_refs[pg][0].astype(BF16), _NT,
                               preferred_element_type=F32)
        pv = part if pv is None else pv + part
    acc_ref[...] = a * acc_ref[...] + pv
    m_ref[...] = m_new

    @pl.when(c == pl.num_programs(1) - 1)
    def _():
        o_ref[0] = jnp.sum(jnp.where(block_diag, acc_ref[...] / l_ref[...], 0.0), axis=0, keepdims=True)


def _decode_attn(page_table, q, k_new, v_new, lf_new, cache_kt, cache_vt, cache_lft, *, nh, pages):
    n, d = q.shape
    dh = d // nh
    page = cache_lft.shape[2]
    n_chunks = page_table.shape[1] // pages

    def tok_spec(width):
        return pl.BlockSpec((1, 1, width), lambda b, c, pt: (b, 0, 0))

    def page_spec(rows, pg):
        return pl.BlockSpec(
            (1, rows, page),
            lambda b, c, pt: (pt[b, (n_chunks - 1 - c) * pages + pg], 0, 0))

    in_specs = [tok_spec(d), tok_spec(d), tok_spec(d), tok_spec(nh)]
    in_specs += [page_spec(d, pg) for pg in range(pages)]
    in_specs += [page_spec(d, pg) for pg in range(pages)]
    in_specs += [page_spec(nh, pg) for pg in range(pages)]
    grid_spec = pltpu.PrefetchScalarGridSpec(
        num_scalar_prefetch=1,
        grid=(n, n_chunks),
        in_specs=in_specs,
        out_specs=tok_spec(d),
        scratch_shapes=[
            pltpu.VMEM((nh, d), BF16), pltpu.VMEM((nh, 1), F32), pltpu.VMEM((nh, 1), F32),
            pltpu.VMEM((nh, d), F32), pltpu.VMEM((nh, 1), F32),
        ],
    )
    out = pl.pallas_call(
        functools.partial(_decode_kernel, pages=pages, nh=nh, dh=dh),
        grid_spec=grid_spec,
        out_shape=jax.ShapeDtypeStruct((n, 1, d), F32),
        compiler_params=_cparams(2),
        name="decode_attn",
    )(page_table, q.reshape(n, 1, d), k_new.reshape(n, 1, d), v_new.reshape(n, 1, d),
      lf_new.reshape(n, 1, nh), *([cache_kt] * pages), *([cache_vt] * pages), *([cache_lft] * pages))
    return out.reshape(n, d)


def _bias_placement(nh, dh, d):
    hpt = LANES // dh
    place = np.zeros((LANES, d), np.float32)
    for h in range(nh):
        col0 = (h // hpt) * LANES + AUG_PER_HEAD * (h % hpt)
        for piece in range(N_SPLIT):
            place[N_SPLIT * nh, col0 + piece] = 1.0
            place[piece * nh + h, col0 + N_SPLIT + piece] = -1.0
    return place


def kernel(x_prompt, x_sample, cache_k, cache_v, cache_logf, state_conv, page_table, a_w_in, a_conv_w, a_w_out, b_w_q, b_w_o, kv_w, f_w, f_b, ffn_w_gu, ffn_w_down, ln_g, ln_b):
    bsz, t, d = x_prompt.shape
    n_s = x_sample.shape[0]
    n_a, n_b = a_w_in.shape[0], b_w_q.shape[0]
    depth = n_a + n_b
    nh = f_w.shape[1]
    dh = d // nh
    n_pool, page = cache_logf.shape[0], cache_logf.shape[1]
    alpha = (2 * depth) ** 0.25
    attn_scale = dh ** -0.5
    q_scale = attn_scale * LOG2E
    tm_seq, tm_fused, tq, dec_pages = 512, 256, 512, 16
    assert LANES % dh == 0 and (N_SPLIT + 1) * nh <= LANES and AUG_PER_HEAD * (LANES // dh) <= LANES

    def layers(stack):
        return [_Layer(stack, l) for l in range(stack.shape[0])]

    w_in, w_out = layers(a_w_in.astype(BF16)), layers(a_w_out.astype(BF16))
    w_q, w_o = layers(b_w_q.astype(BF16)), layers(b_w_o.astype(BF16))
    w_qt = layers(jnp.swapaxes(b_w_q, 1, 2).astype(BF16))
    w_kv = kv_w.astype(BF16)
    w_kvt = kv_w.T.astype(BF16)
    w_gu, w_down = layers(ffn_w_gu.astype(BF16)), layers(ffn_w_down.astype(BF16))
    fw_pad = jnp.pad(f_w, ((0, 0), (0, LANES - nh))).astype(BF16)
    fb_pad = jnp.pad(f_b, (0, LANES - nh)).reshape(1, LANES)
    fw_rep = jnp.pad(jnp.tile(f_w, (1, N_SPLIT)), ((0, 0), (0, LANES - N_SPLIT * nh))).astype(BF16)
    fb_rep = jnp.pad(jnp.tile(f_b, N_SPLIT), (0, LANES - N_SPLIT * nh)).reshape(1, LANES)
    fwt = f_w.T.astype(BF16)
    fbt = f_b.reshape(nh, 1)
    place = jnp.asarray(_bias_placement(nh, dh, d), BF16)
    g = ln_g.reshape(depth, 2, 1, d)
    b = ln_b.reshape(depth, 2, 1, d)

    x = x_prompt
    hist0 = jnp.zeros((bsz, 2, d), F32)
    conv_p = []
    for l in range(n_a):
        x, st = _conv_seq(x, hist0, w_in[l], a_conv_w[l], w_out[l], g[l, 0], b[l, 0], alpha=alpha, tm=tm_seq)
        conv_p.append(st)
        x = _ffn(x.reshape(bsz * t, d), w_gu[l], w_down[l], g[l, 1], b[l, 1], alpha=alpha, tm=tm_seq).reshape(bsz, t, d)
    kt_p, vt_p, lft_p, kb, ak, vtb, ct, qt = _kv_seq(x, w_kvt, fw_rep, fb_rep, fwt, fbt, place,
                                                     w_qt[0], nh=nh, tm=tm_seq, q_scale=q_scale)
    for j in range(n_b):
        l = n_a + j
        o = _flash_attn(qt, kb, ak, vtb, ct, nh=nh, tq=tq)
        w_next = w_qt[j + 1] if j + 1 < n_b else None
        out = _attn_ffn(o, x, w_o[j], g[l, 0], b[l, 0], w_gu[l], w_down[l], g[l, 1], b[l, 1], w_next,
                        alpha=alpha, tm=tm_fused, q_scale=q_scale)
        x, qt = out if w_next is not None else (out, None)
    y_prompt = x

    cache_kt = jnp.transpose(cache_k, (0, 2, 3, 1)).reshape(n_pool, d, page)
    cache_vt = jnp.transpose(cache_v, (0, 2, 3, 1)).reshape(n_pool, d, page)
    cache_lft = jnp.transpose(cache_logf, (0, 2, 1))
    xs = x_sample.reshape(n_s, d)
    conv_s = []
    for l in range(n_a):
        xs, st = _conv_tok(xs, state_conv[l].reshape(n_s, 2 * d), w_in[l], a_conv_w[l], w_out[l],
                           g[l, 0], b[l, 0], alpha=alpha)
        conv_s.append(st.reshape(n_s, 2, d))
        xs = _ffn(xs, w_gu[l], w_down[l], g[l, 1], b[l, 1], alpha=alpha, tm=n_s)
    k_s, v_s, lf_s = _kv_tok(xs, w_kv, fw_pad, fb_pad, nh=nh)
    for j in range(n_b):
        l = n_a + j
        qs = _proj(xs, w_q[j], scale=attn_scale, out_dtype=F32, tm=n_s)
        os_ = _decode_attn(page_table, qs, k_s, v_s, lf_s, cache_kt, cache_vt, cache_lft, nh=nh, pages=dec_pages)
        xs = _proj_ln(os_, xs, w_o[j], g[l, 0], b[l, 0], alpha=alpha, tm=n_s)
        xs = _ffn(xs, w_gu[l], w_down[l], g[l, 1], b[l, 1], alpha=alpha, tm=n_s)

    return (y_prompt, xs.reshape(n_s, 1, d),
            jnp.transpose(kt_p, (0, 3, 1, 2)), jnp.transpose(vt_p, (0, 3, 1, 2)), jnp.transpose(lft_p, (0, 2, 1)),
            jnp.stack(conv_p),
            k_s.reshape(n_s, 1, nh, dh), v_s.reshape(n_s, 1, nh, dh), lf_s.reshape(n_s, 1, nh),
            jnp.stack(conv_s))
```
